```python
import math
import jax, jax.numpy as jnp
from jax import lax
import numpy as np

D_MODEL = 2048
BATCH = 1
SEQ = 8192
DEPTH = 1
DEC_BATCH = 32
DEC_SEQ = 1
PAST_LEN = 8192
PAGE_SIZE = 128

D_ATTN = D_MODEL // 2
N_DIFF_HEADS = 8
DH = D_ATTN // (2 * N_DIFF_HEADS)
DV = 2 * DH
D_SG = D_MODEL // 2
N_SG_GROUPS = 8
SG_GROUP = D_SG // N_SG_GROUPS
CHUNK = 128
Q_BLOCK = 128
ROPE_THETA = 10000.0
EPS = 1e-6
NEG = -1e30
D_IN = 4 * D_ATTN + 3 * D_SG + 2 * D_MODEL

kernel_name = 'hybrid_diffattn_chunkgmlp_gated_decoder_step'

F32 = jnp.float32


def _rmsnorm(x, g):
    xf = x.astype(F32)
    y = xf * lax.rsqrt(jnp.mean(xf * xf, axis=-1, keepdims=True) + EPS) * g.astype(F32)
    return y.astype(x.dtype)


def _layernorm(x, g, b):
    xf = x.astype(F32)
    mu = jnp.mean(xf, axis=-1, keepdims=True)
    xc = xf - mu
    var = jnp.mean(xc * xc, axis=-1, keepdims=True)
    y = xc * lax.rsqrt(var + EPS) * g.astype(F32) + b.astype(F32)
    return y.astype(x.dtype)


def _split_in(z):
    sizes = (D_ATTN, D_ATTN, D_ATTN, D_ATTN, D_SG, D_SG, D_SG, D_MODEL, D_MODEL)
    idx = [int(i) for i in np.cumsum(sizes)[:-1]]
    return jnp.split(z, idx, axis=-1)


def _rope(x, pos):
    half = DH // 2
    inv = ROPE_THETA ** (-(jnp.arange(half, dtype=F32) * 2.0 / DH))
    ang = pos.astype(F32)[:, None] * inv[None, :]
    cos = jnp.cos(ang)[:, None, None, :]
    sin = jnp.sin(ang)[:, None, None, :]
    xf = x.astype(F32)
    x1, x2 = xf[..., :half], xf[..., half:]
    return jnp.concatenate([x1 * cos - x2 * sin, x2 * cos + x1 * sin], axis=-1).astype(x.dtype)


def _diff_attn_prompt(q, k, v, lam):
    b, s = q.shape[0], q.shape[1]
    nb = s // Q_BLOCK
    scale = DH ** -0.5
    kf = k.astype(F32)
    vf = v.astype(F32)
    qb = jnp.moveaxis(q.reshape(b, nb, Q_BLOCK, N_DIFF_HEADS, 2, DH), 1, 0)
    kpos = jnp.arange(s)

    def block(args):
        q_blk, i = args
        sc = jnp.einsum('bqhmd,bkhmd->bhmqk', q_blk.astype(F32) * scale, kf)
        qpos = i * Q_BLOCK + jnp.arange(Q_BLOCK)
        mask = kpos[None, :] <= qpos[:, None]
        p = jax.nn.softmax(jnp.where(mask, sc, NEG), axis=-1)
        w = p[:, :, 0] - lam * p[:, :, 1]
        return jnp.einsum('bhqk,bkhe->bqhe', w, vf)

    o = lax.map(block, (qb, jnp.arange(nb)))
    return jnp.moveaxis(o, 0, 1).reshape(b, s, N_DIFF_HEADS, DV)


def _diff_attn_sample(q, k_new, v_new, k_past, v_past, lam):
    scale = DH ** -0.5
    qf = q.astype(F32) * scale
    s_past = jnp.einsum('bqhmd,bkhmd->bhmqk', qf, k_past.astype(F32))
    s_new = jnp.einsum('bqhmd,bkhmd->bhmqk', qf, k_new.astype(F32))
    t = q.shape[1]
    causal = jnp.tril(jnp.ones((t, t), dtype=bool))
    s_new = jnp.where(causal, s_new, NEG)
    p = jax.nn.softmax(jnp.concatenate([s_past, s_new], axis=-1), axis=-1)
    w = p[:, :, 0] - lam * p[:, :, 1]
    n_past = k_past.shape[1]
    return (jnp.einsum('bhqk,bkhe->bqhe', w[..., :n_past], v_past.astype(F32))
            + jnp.einsum('bhqk,bkhe->bqhe', w[..., n_past:], v_new.astype(F32)))


def _diff_branch_out(o, sub_g, lam_init, gate_a, w_pa):
    on = _rmsnorm(o, sub_g) * (1.0 - lam_init)
    on = on.reshape(on.shape[0], on.shape[1], D_ATTN).astype(gate_a.dtype)
    return (on * jax.nn.silu(gate_a)) @ w_pa


def _spatial_gate(vc, wm, bs):
    return jnp.einsum('gts,bcsgd->bctgd', wm, vc) + jnp.transpose(bs)[None, None, :, :, None]


def _merge(x, ya, yb, ga, gb, w_o, norm_post):
    m = jax.nn.sigmoid(ga) * ya + jax.nn.sigmoid(gb) * yb
    return x + _rmsnorm(m @ w_o, norm_post).astype(x.dtype)


def setup_inputs(seed: int = 0) -> dict:
    key = jax.random.key(seed)
    ks = jax.random.split(key, 24)
    n_pages = PAST_LEN // PAGE_SIZE
    n_used = DEC_BATCH * n_pages
    n_pool = n_used + max(1, n_used // 4)
    perm = jax.random.permutation(ks[0], n_pool)
    page_table = perm[:n_used].reshape(DEC_BATCH, n_pages).astype(jnp.int32)
    nrm = lambda k, shp, sc: jax.random.normal(k, shp, F32) * sc
    return {
        'x_prompt': nrm(ks[1], (BATCH, SEQ, D_MODEL), 1.0),
        'x_sample': nrm(ks[2], (DEC_BATCH, DEC_SEQ, D_MODEL), 1.0),
        'cache_k': nrm(ks[3], (DEPTH, n_pool, PAGE_SIZE, N_DIFF_HEADS, 2, DH), 1.0),
        'cache_v': nrm(ks[4], (DEPTH, n_pool, PAGE_SIZE, N_DIFF_HEADS, DV), 1.0),
        'page_table': page_table,
        'norm_pre': 1.0 + nrm(ks[5], (DEPTH, D_MODEL), 0.02),
        'w_in': nrm(ks[6], (DEPTH, D_MODEL, D_IN), D_MODEL ** -0.5),
        'lam_q1': nrm(ks[7], (DEPTH, DH), 0.1),
        'lam_k1': nrm(ks[8], (DEPTH, DH), 0.1),
        'lam_q2': nrm(ks[9], (DEPTH, DH), 0.1),
        'lam_k2': nrm(ks[10], (DEPTH, DH), 0.1),
        'sub_g': 1.0 + nrm(ks[11], (DEPTH, DV), 0.02),
        'w_pa': nrm(ks[12], (DEPTH, D_ATTN, D_MODEL), D_ATTN ** -0.5),
        'ln_g': 1.0 + nrm(ks[13], (DEPTH, D_SG), 0.02),
        'ln_b': nrm(ks[14], (DEPTH, D_SG), 0.02),
        'w_s': nrm(ks[15], (DEPTH, N_SG_GROUPS, CHUNK, CHUNK), CHUNK ** -0.5),
        'b_s': nrm(ks[16], (DEPTH, N_SG_GROUPS, CHUNK), 0.02),
        'w_pb': nrm(ks[17], (DEPTH, D_SG, D_MODEL), D_SG ** -0.5),
        'w_o': nrm(ks[18], (DEPTH, D_MODEL, D_MODEL), D_MODEL ** -0.5),
        'norm_post': 1.0 + nrm(ks[19], (DEPTH, D_MODEL), 0.02),
    }


def reference(x_prompt, x_sample, cache_k, cache_v, page_table, norm_pre, w_in, lam_q1, lam_k1,
              lam_q2, lam_k2, sub_g, w_pa, ln_g, ln_b, w_s, b_s, w_pb, w_o, norm_post):
    x_p, x_s = x_prompt, x_sample
    b, s = x_p.shape[0], x_p.shape[1]
    db, t = x_s.shape[0], x_s.shape[1]
    n_past = page_table.shape[1] * cache_k.shape[2]
    pos_p = jnp.arange(s, dtype=jnp.int32)
    pos_s = n_past + jnp.arange(t, dtype=jnp.int32)
    tril = jnp.tril(jnp.ones((CHUNK, CHUNK), dtype=w_s.dtype))
    kp_l, vp_l, ks_l, vs_l, sg_l = [], [], [], [], []
    for l in range(DEPTH):
        lam_init = 0.8 - 0.6 * math.exp(-0.3 * l)
        lam = (jnp.exp(jnp.sum(lam_q1[l].astype(F32) * lam_k1[l].astype(F32)))
               - jnp.exp(jnp.sum(lam_q2[l].astype(F32) * lam_k2[l].astype(F32))) + lam_init)
        wm = w_s[l] * tril

        q, k, v, gate_a, u, vsg, gate_b, g_a, g_b = _split_in(_rmsnorm(x_p, norm_pre[l]) @ w_in[l])
        q = _rope(q.reshape(b, s, N_DIFF_HEADS, 2, DH), pos_p)
        k = _rope(k.reshape(b, s, N_DIFF_HEADS, 2, DH), pos_p)
        v = v.reshape(b, s, N_DIFF_HEADS, DV)
        ya = _diff_branch_out(_diff_attn_prompt(q, k, v, lam), sub_g[l], lam_init, gate_a, w_pa[l])
        vn = _layernorm(vsg, ln_g[l], ln_b[l])
        sp = _spatial_gate(vn.reshape(b, s // CHUNK, CHUNK, N_SG_GROUPS, SG_GROUP), wm, b_s[l])
        yb = (u * sp.reshape(b, s, D_SG).astype(u.dtype) * jax.nn.silu(gate_b)) @ w_pb[l]
        x_p = _merge(x_p, ya, yb, g_a, g_b, w_o[l], norm_post[l])
        kp_l.append(k)
        vp_l.append(v)

        q, k, v, gate_a, u, vsg, gate_b, g_a, g_b = _split_in(_rmsnorm(x_s, norm_pre[l]) @ w_in[l])
        q = _rope(q.reshape(db, t, N_DIFF_HEADS, 2, DH), pos_s)
        k = _rope(k.reshape(db, t, N_DIFF_HEADS, 2, DH), pos_s)
        v = v.reshape(db, t, N_DIFF_HEADS, DV)
        k_past = cache_k[l][page_table].reshape(db, n_past, N_DIFF_HEADS, 2, DH)
        v_past = cache_v[l][page_table].reshape(db, n_past, N_DIFF_HEADS, DV)
        o = _diff_attn_sample(q, k, v, k_past, v_past, lam)
        ya = _diff_branch_out(o, sub_g[l], lam_init, gate_a, w_pa[l])
        vn = _layernorm(vsg, ln_g[l], ln_b[l])
        sp = _spatial_gate(vn.reshape(db, 1, t, N_SG_GROUPS, SG_GROUP), wm[:, :t, :t], b_s[l][:, :t])
        yb = (u * sp.reshape(db, t, D_SG).astype(u.dtype) * jax.nn.silu(gate_b)) @ w_pb[l]
        x_s = _merge(x_s, ya, yb, g_a, g_b, w_o[l], norm_post[l])
        ks_l.append(k)
        vs_l.append(v)
        sg_l.append(vn)

    new_k_prompt = jnp.stack(kp_l)
    new_v_prompt = jnp.stack(vp_l)
    new_k_sample = jnp.stack(ks_l)
    new_v_sample = jnp.stack(vs_l)
    new_sgv_sample = jnp.stack(sg_l)
    return (x_p, x_s, new_k_prompt, new_v_prompt, new_k_sample, new_v_sample, new_sgv_sample)
```

```python
import functools
import math

import jax
import jax.numpy as jnp
from jax import lax
from jax.experimental import pallas as pl
from jax.experimental.pallas import tpu as pltpu

F32 = jnp.float32
BF16 = jnp.bfloat16

D_MODEL = 2048
D_ATTN = D_MODEL // 2
N_HEADS = 8
DH = D_ATTN // (2 * N_HEADS)
DV = 2 * DH
D_SG = D_MODEL // 2
N_GROUPS = 8
SG_GROUP = D_SG // N_GROUPS
CHUNK = 128
ROPE_THETA = 10000.0
EPS = 1e-6
NEG = -1e30
D_IN = 4 * D_ATTN + 3 * D_SG + 2 * D_MODEL
COL_BLOCK = 1024
N_COL_BLOCKS = D_IN // COL_BLOCK
LANES = 128
SCALE = DH ** -0.5
LAM_INIT = 0.8 - 0.6 * math.exp(-0.3 * 0)
MIB = 1024 * 1024


def _silu(x):
    return x * jax.nn.sigmoid(x)


def _lam(lq1, lk1, lq2, lk2):
    a = jnp.sum(lq1[...] * lk1[...], axis=-1, keepdims=True)
    b = jnp.sum(lq2[...] * lk2[...], axis=-1, keepdims=True)
    return jnp.exp(a) - jnp.exp(b) + LAM_INIT


def _inproj_kernel(x_ref, g_ref, w_ref, cos_ref, sin_ref, z_ref, k_ref, v_ref, xn_ref):
    j = pl.program_id(1)
    tm = x_ref.shape[0]

    @pl.when(j == 0)
    def _():
        x = x_ref[...]
        ms = jnp.mean(x * x, axis=-1, keepdims=True)
        xn_ref[...] = (x * lax.rsqrt(ms + EPS) * g_ref[...]).astype(BF16)

    acc = jnp.dot(xn_ref[...], w_ref[...], preferred_element_type=F32)

    def rope(a):
        cos = cos_ref[...]
        sin = sin_ref[...]
        lane = lax.broadcasted_iota(jnp.int32, (tm, LANES), 1)
        first_half = (lane % DH) < (DH // 2)
        outs = []
        for c in range(COL_BLOCK // LANES):
            blk = a[:, c * LANES:(c + 1) * LANES]
            partner = jnp.where(first_half,
                                pltpu.roll(blk, LANES - DH // 2, 1),
                                pltpu.roll(blk, DH // 2, 1))
            outs.append(blk * cos + partner * sin)
        return jnp.concatenate(outs, axis=1)

    @pl.when(j == 0)
    def _():
        z_ref[...] = (rope(acc) * SCALE).astype(z_ref.dtype)

    @pl.when(j == 1)
    def _():
        r = rope(acc)
        z_ref[...] = r.astype(z_ref.dtype)
        k_ref[...] = r

    @pl.when(j == 2)
    def _():
        z_ref[...] = acc.astype(z_ref.dtype)
        v_ref[...] = acc

    @pl.when(j > 2)
    def _():
        z_ref[...] = acc.astype(z_ref.dtype)


def _inproj(x, g, w_bf, cos_t, sin_t, *, tm, z_dtype):
    m = x.shape[0]
    return pl.pallas_call(
        _inproj_kernel,
        grid=(m // tm, N_COL_BLOCKS),
        in_specs=[
            pl.BlockSpec((tm, D_MODEL), lambda i, j: (i, 0)),
            pl.BlockSpec((1, D_MODEL), lambda i, j: (0, 0)),
            pl.BlockSpec((D_MODEL, COL_BLOCK), lambda i, j: (0, j)),
            pl.BlockSpec((tm, LANES), lambda i, j: (i, 0)),
            pl.BlockSpec((tm, LANES), lambda i, j: (i, 0)),
        ],
        out_specs=[
            pl.BlockSpec((tm, COL_BLOCK), lambda i, j: (i, j)),
            pl.BlockSpec((tm, COL_BLOCK), lambda i, j: (i, 0)),
            pl.BlockSpec((tm, COL_BLOCK), lambda i, j: (i, 0)),
        ],
        out_shape=[
            jax.ShapeDtypeStruct((m, D_IN), z_dtype),
            jax.ShapeDtypeStruct((m, D_ATTN), F32),
            jax.ShapeDtypeStruct((m, D_ATTN), F32),
        ],
        scratch_shapes=[pltpu.VMEM((tm, D_MODEL), BF16)],
        compiler_params=pltpu.CompilerParams(
            dimension_semantics=("parallel", "arbitrary"),
            vmem_limit_bytes=48 * MIB),
        name="inproj",
    )(x, g, w_bf, cos_t, sin_t)


def _prompt_attn_kernel(q_ref, k_ref, v_ref, ga_ref, lq1, lk1, lq2, lk2, subg_ref, o_ref,
                        m_ref, l_ref, acc_ref, *, tq):
    qi = pl.program_id(1)
    q = q_ref[...]
    lane = lax.broadcasted_iota(jnp.int32, (tq, LANES), 1)
    zero = jnp.zeros_like(q)
    qq = jnp.concatenate([jnp.where(lane < DH, q, zero), jnp.where(lane >= DH, q, zero)], axis=0)

    m_ref[...] = jnp.full(m_ref.shape, NEG, F32)
    l_ref[...] = jnp.zeros(l_ref.shape, F32)
    acc_ref[...] = jnp.zeros(acc_ref.shape, F32)

    def step(kb, masked):
        start = pl.multiple_of(kb * tq, tq)
        ks = k_ref[pl.ds(start, tq), :]
        vs = v_ref[pl.ds(start, tq), :]
        s = lax.dot_general(qq, ks, (((1,), (1,)), ((), ())), preferred_element_type=F32)
        if masked:
            row = lax.broadcasted_iota(jnp.int32, (2 * tq, tq), 0)
            col = lax.broadcasted_iota(jnp.int32, (2 * tq, tq), 1)
            row = jnp.where(row >= tq, row - tq, row)
            s = jnp.where(col <= row, s, NEG)
        m_old = m_ref[...]
        m_new = jnp.maximum(m_old, jnp.max(s, axis=-1, keepdims=True))
        alpha = jnp.exp(m_old - m_new)
        p = jnp.exp(s - m_new)
        l_ref[...] = alpha * l_ref[...] + jnp.sum(p, axis=-1, keepdims=True)
        acc_ref[...] = alpha * acc_ref[...] + jnp.dot(p.astype(BF16), vs,
                                                      preferred_element_type=F32)
        m_ref[...] = m_new

    def body(kb, carry):
        step(kb, False)
        return carry

    lax.fori_loop(0, qi, body, 0)
    step(qi, True)

    lam = _lam(lq1, lk1, lq2, lk2)
    o0 = acc_ref[:tq, :] / l_ref[:tq, :]
    o1 = acc_ref[tq:, :] / l_ref[tq:, :]
    o = o0 - lam * o1
    ms = jnp.mean(o * o, axis=-1, keepdims=True)
    on = o * lax.rsqrt(ms + EPS) * subg_ref[...] * (1.0 - LAM_INIT)
    ga = ga_ref[...].astype(F32)
    o_ref[...] = (on * _silu(ga)).astype(o_ref.dtype)


def _prompt_attn(z, lq1, lk1, lq2, lk2, subg, *, tq):
    s = z.shape[0]
    small = lambda n: pl.BlockSpec((1, n), lambda h, i: (0, 0))
    return pl.pallas_call(
        functools.partial(_prompt_attn_kernel, tq=tq),
        grid=(N_HEADS, s // tq),
        in_specs=[
            pl.BlockSpec((tq, LANES), lambda h, i: (i, h)),
            pl.BlockSpec((s, LANES), lambda h, i: (0, N_HEADS + h)),
            pl.BlockSpec((s, LANES), lambda h, i: (0, 2 * N_HEADS + h)),
            pl.BlockSpec((tq, LANES), lambda h, i: (i, 3 * N_HEADS + h)),
            small(DH), small(DH), small(DH), small(DH), small(DV),
        ],
        out_specs=pl.BlockSpec((tq, LANES), lambda h, i: (i, h)),
        out_shape=jax.ShapeDtypeStruct((s, D_ATTN), BF16),
        scratch_shapes=[pltpu.VMEM((2 * tq, 1), F32), pltpu.VMEM((2 * tq, 1), F32),
                        pltpu.VMEM((2 * tq, DV), F32)],
        compiler_params=pltpu.CompilerParams(
            dimension_semantics=("parallel", "arbitrary"),
            vmem_limit_bytes=48 * MIB),
        name="prompt_attn",
    )(z, z, z, z, lq1, lk1, lq2, lk2, subg)


def _sample_attn_kernel(pt_ref, qrow_ref, qcol_ref, kn_ref, vn_ref, ga_ref, lq1, lk1, lq2, lk2,
                        subg_ref, expand_ref, *rest, pages, n_chunks):
    del pt_ref
    k_pages = rest[:pages]
    v_pages = rest[pages:2 * pages]
    o_ref = rest[2 * pages]
    sc_ref, p_ref, pn_ref, acc_ref = rest[2 * pages + 1:]
    c = pl.program_id(1)
    n_maps = 2 * N_HEADS

    @pl.when(c < n_chunks)
    def _():
        qcol = qcol_ref[0]
        for j, kp in enumerate(k_pages):
            prod = (kp[...] * qcol).reshape(N_HEADS, 2 * DH, CHUNK)
            lanes = slice(j * CHUNK, (j + 1) * CHUNK)
            sc_ref[c, :N_HEADS, lanes] = jnp.sum(prod[:, :DH, :], axis=1)
            sc_ref[c, N_HEADS:, lanes] = jnp.sum(prod[:, DH:, :], axis=1)

    @pl.when(c == n_chunks - 1)
    def _():
        s = sc_ref[...]
        row = lax.broadcasted_iota(jnp.int32, (n_maps, D_ATTN), 0)
        seg = lax.broadcasted_iota(jnp.int32, (n_maps, D_ATTN), 1) // DH
        want = jnp.where(row < N_HEADS, 2 * row, 2 * (row - N_HEADS) + 1)
        qk_new = jnp.broadcast_to(qrow_ref[0] * kn_ref[0], (n_maps, D_ATTN))
        s_new = jnp.sum(jnp.where(seg == want, qk_new, 0.0), axis=-1, keepdims=True)
        m = jnp.max(jnp.max(s, axis=-1, keepdims=True), axis=0)
        m = jnp.maximum(m, s_new)
        p = jnp.exp(s - m[None])
        pn = jnp.exp(s_new - m)
        l = jnp.sum(jnp.sum(p, axis=-1, keepdims=True), axis=0) + pn
        inv_l = 1.0 / l
        p_ref[...] = (p * inv_l[None]).astype(BF16)
        pn_ref[...] = pn * inv_l
        acc_ref[...] = jnp.zeros(acc_ref.shape, F32)

    @pl.when(c >= n_chunks)
    def _():
        cv = c - n_chunks
        pstack = jnp.concatenate(
            [p_ref[cv, :, j * CHUNK:(j + 1) * CHUNK] for j in range(pages)], axis=0)
        pexp = jnp.dot(pstack, expand_ref[...], preferred_element_type=F32)
        row = lax.broadcasted_iota(jnp.int32, pexp.shape, 0)
        col = lax.broadcasted_iota(jnp.int32, pexp.shape, 1)
        pexp = jnp.where(row % N_HEADS == col % N_HEADS, pexp, 0.0).astype(BF16)
        acc = acc_ref[...]
        for j, vp in enumerate(v_pages):
            acc += jnp.dot(pexp[j * n_maps:(j + 1) * n_maps, :], vp[...].astype(BF16),
                           preferred_element_type=F32)
        acc_ref[...] = acc

    @pl.when(c == 2 * n_chunks - 1)
    def _():
        lam = _lam(lq1, lk1, lq2, lk2)
        v_new = vn_ref[0]
        full = acc_ref[...] + pn_ref[...] * jnp.concatenate([v_new, v_new], axis=0)
        o = full[:N_HEADS, :] - lam * full[N_HEADS:, :]
        ms = jnp.mean(o * o, axis=-1, keepdims=True)
        on = o * lax.rsqrt(ms + EPS) * subg_ref[...] * (1.0 - LAM_INIT)
        o_ref[0] = (on * _silu(ga_ref[0])).astype(o_ref.dtype)


def _sample_attn(page_table, q_s, k_s, v_s, ga_s, cache_k, cache_v, lq1, lk1, lq2, lk2, subg,
                 *, pages):
    db, n_pages = page_table.shape
    n_chunks = n_pages // pages
    span = pages * CHUNK
    n_pool = cache_k.shape[0]
    ck = jnp.transpose(cache_k, (0, 2, 3, 4, 1)).reshape(n_pool, D_ATTN, CHUNK)
    cv = cache_v.reshape(n_pool, CHUNK * N_HEADS, DV)
    pt = page_table.reshape(-1)
    qrow = q_s.reshape(db, 1, D_ATTN)
    qcol = jnp.broadcast_to(q_s[:, :, None], (db, D_ATTN, LANES))
    knew = k_s.reshape(db, 1, D_ATTN)
    vnew = v_s.reshape(db, N_HEADS, DV)
    ga = ga_s.reshape(db, N_HEADS, DV)
    expand = (jnp.arange(CHUNK * N_HEADS)[None, :] // N_HEADS
              == jnp.arange(CHUNK)[:, None]).astype(BF16)

    per_b = lambda shp: pl.BlockSpec((1,) + shp, lambda b, c, pt_: (b, 0, 0))

    def kspec(j):
        return pl.BlockSpec(
            (None, D_ATTN, CHUNK),
            lambda b, c, pt_: (pt_[b * n_pages + jnp.minimum(c, n_chunks - 1) * pages + j], 0, 0))

    def vspec(j):
        return pl.BlockSpec(
            (None, CHUNK * N_HEADS, DV),
            lambda b, c, pt_: (pt_[b * n_pages + jnp.maximum(c - n_chunks, 0) * pages + j], 0, 0))

    small = lambda n: pl.BlockSpec((1, n), lambda b, c, pt_: (0, 0))
    grid_spec = pltpu.PrefetchScalarGridSpec(
        num_scalar_prefetch=1,
        grid=(db, 2 * n_chunks),
        in_specs=[per_b((1, D_ATTN)), per_b((D_ATTN, LANES)), per_b((1, D_ATTN)),
                  per_b((N_HEADS, DV)), per_b((N_HEADS, DV)),
                  small(DH), small(DH), small(DH), small(DH), small(DV),
                  pl.BlockSpec((CHUNK, CHUNK * N_HEADS), lambda b, c, pt_: (0, 0))]
                 + [kspec(j) for j in range(pages)] + [vspec(j) for j in range(pages)],
        out_specs=pl.BlockSpec((1, N_HEADS, DV), lambda b, c, pt_: (b, 0, 0)),
        scratch_shapes=[pltpu.VMEM((n_chunks, 2 * N_HEADS, span), F32),
                        pltpu.VMEM((n_chunks, 2 * N_HEADS, span), BF16),
                        pltpu.VMEM((2 * N_HEADS, 1), F32),
                        pltpu.VMEM((2 * N_HEADS, DV), F32)],
    )
    out = pl.pallas_call(
        functools.partial(_sample_attn_kernel, pages=pages, n_chunks=n_chunks),
        grid_spec=grid_spec,
        out_shape=jax.ShapeDtypeStruct((db, N_HEADS, DV), BF16),
        compiler_params=pltpu.CompilerParams(
            dimension_semantics=("parallel", "arbitrary"),
            vmem_limit_bytes=48 * MIB),
        name="sample_attn",
    )(pt, qrow, qcol, knew, vnew, ga, lq1, lk1, lq2, lk2, subg, expand,
      *([ck] * pages), *([cv] * pages))
    return out.reshape(db, D_ATTN)


def _layernorm(x, g, b):
    mu = jnp.mean(x, axis=-1, keepdims=True)
    xc = x - mu
    var = jnp.mean(xc * xc, axis=-1, keepdims=True)
    return xc * lax.rsqrt(var + EPS) * g + b


def _sgate_prompt_kernel(u_ref, vsg_ref, gb_ref, lng_ref, lnb_ref, ws_ref, bias_ref, o_ref):
    rows = u_ref.shape[0]
    n_c = rows // CHUNK
    vn = _layernorm(vsg_ref[...].astype(F32), lng_ref[...], lnb_ref[...]).astype(BF16)
    t = lax.broadcasted_iota(jnp.int32, (CHUNK, CHUNK), 0)
    s = lax.broadcasted_iota(jnp.int32, (CHUNK, CHUNK), 1)
    causal = s <= t
    for g in range(N_GROUPS):
        cols = slice(g * SG_GROUP, (g + 1) * SG_GROUP)
        wm = jnp.where(causal, ws_ref[g], 0.0).astype(BF16)
        rhs = jnp.concatenate([vn[c * CHUNK:(c + 1) * CHUNK, cols] for c in range(n_c)], axis=1)
        sp = jnp.dot(wm, rhs, preferred_element_type=F32)
        for c in range(n_c):
            rws = slice(c * CHUNK, (c + 1) * CHUNK)
            sp_c = sp[:, c * SG_GROUP:(c + 1) * SG_GROUP] + bias_ref[:, cols]
            gb = gb_ref[rws, cols].astype(F32)
            o_ref[rws, cols] = (u_ref[rws, cols].astype(F32) * sp_c * _silu(gb)).astype(o_ref.dtype)


def _sgate_prompt(z, ln_g, ln_b, w_s, bias, *, rows):
    s = z.shape[0]
    const2 = lambda shp: pl.BlockSpec(shp, lambda i: (0, 0))
    return pl.pallas_call(
        _sgate_prompt_kernel,
        grid=(s // rows,),
        in_specs=[
            pl.BlockSpec((rows, D_SG), lambda i: (i, 4)),
            pl.BlockSpec((rows, D_SG), lambda i: (i, 5)),
            pl.BlockSpec((rows, D_SG), lambda i: (i, 6)),
            const2((1, D_SG)), const2((1, D_SG)),
            pl.BlockSpec((N_GROUPS, CHUNK, CHUNK), lambda i: (0, 0, 0)),
            const2((CHUNK, D_SG)),
        ],
        out_specs=pl.BlockSpec((rows, D_SG), lambda i: (i, 0)),
        out_shape=jax.ShapeDtypeStruct((s, D_SG), BF16),
        compiler_params=pltpu.CompilerParams(dimension_semantics=("parallel",),
                                             vmem_limit_bytes=48 * MIB),
        name="sgate_prompt",
    )(z, z, z, ln_g, ln_b, w_s, bias)


def _sgate_sample_kernel(u_ref, vsg_ref, gb_ref, lng_ref, lnb_ref, w0_ref, b0_ref, o_ref, vn_ref):
    vn = _layernorm(vsg_ref[...], lng_ref[...], lnb_ref[...])
    vn_ref[...] = vn
    sp = w0_ref[...] * vn + b0_ref[...]
    o_ref[...] = (u_ref[...] * sp * _silu(gb_ref[...])).astype(o_ref.dtype)


def _sgate_sample(zs, ln_g, ln_b, w0, b0):
    db = zs.shape[0]
    const2 = lambda shp: pl.BlockSpec(shp, lambda i: (0, 0))
    return pl.pallas_call(
        _sgate_sample_kernel,
        grid=(1,),
        in_specs=[
            pl.BlockSpec((db, D_SG), lambda i: (0, 4)),
            pl.BlockSpec((db, D_SG), lambda i: (0, 5)),
            pl.BlockSpec((db, D_SG), lambda i: (0, 6)),
            const2((1, D_SG)), const2((1, D_SG)), const2((1, D_SG)), const2((1, D_SG)),
        ],
        out_specs=[pl.BlockSpec((db, D_SG), lambda i: (0, 0)),
                   pl.BlockSpec((db, D_SG), lambda i: (0, 0))],
        out_shape=[jax.ShapeDtypeStruct((db, D_SG), BF16),
                   jax.ShapeDtypeStruct((db, D_SG), F32)],
        name="sgate_sample",
    )(zs, zs, zs, ln_g, ln_b, w0, b0)


def _merge_kernel(x_ref, a_ref, b_ref, ga0_ref, ga1_ref, gb0_ref, gb1_ref,
                  wpa_ref, wpb_ref, wo_ref, np_ref, o_ref):
    ya = jnp.dot(a_ref[...], wpa_ref[...], preferred_element_type=F32)
    yb = jnp.dot(b_ref[...], wpb_ref[...], preferred_element_type=F32)
    half = D_MODEL // 2
    sig = lambda r: jax.nn.sigmoid(r[...].astype(F32))
    m_lo = sig(ga0_ref) * ya[:, :half] + sig(gb0_ref) * yb[:, :half]
    m_hi = sig(ga1_ref) * ya[:, half:] + sig(gb1_ref) * yb[:, half:]
    m = jnp.concatenate([m_lo, m_hi], axis=1).astype(BF16)
    mo = jnp.dot(m, wo_ref[...], preferred_element_type=F32)
    ms = jnp.mean(mo * mo, axis=-1, keepdims=True)
    o_ref[...] = x_ref[...] + mo * lax.rsqrt(ms + EPS) * np_ref[...]


def _merge(x, a_in, b_in, z, wpa_bf, wpb_bf, wo_bf, norm_post, *, tm):
    m = x.shape[0]
    resident = lambda shp: pl.BlockSpec(shp, lambda i: (0, 0), pipeline_mode=pl.Buffered(1))
    zcol = lambda jcol: pl.BlockSpec((tm, COL_BLOCK), lambda i: (i, jcol))
    return pl.pallas_call(
        _merge_kernel,
        grid=(m // tm,),
        in_specs=[
            pl.BlockSpec((tm, D_MODEL), lambda i: (i, 0)),
            pl.BlockSpec((tm, D_ATTN), lambda i: (i, 0)),
            pl.BlockSpec((tm, D_SG), lambda i: (i, 0)),
            zcol(7), zcol(8), zcol(9), zcol(10),
            resident((D_ATTN, D_MODEL)), resident((D_SG, D_MODEL)), resident((D_MODEL, D_MODEL)),
            resident((1, D_MODEL)),
        ],
        out_specs=pl.BlockSpec((tm, D_MODEL), lambda i: (i, 0)),
        out_shape=jax.ShapeDtypeStruct((m, D_MODEL), F32),
        compiler_params=pltpu.CompilerParams(dimension_semantics=("parallel",),
                                             vmem_limit_bytes=56 * MIB),
        name="merge",
    )(x, a_in, b_in, z, z, z, z, wpa_bf, wpb_bf, wo_bf, norm_post)


def _rope_tables(pos):
    half = DH // 2
    inv = ROPE_THETA ** (-(jnp.arange(half, dtype=F32) * 2.0 / DH))
    ang = pos.astype(F32)[:, None] * inv[None, :]
    cos = jnp.cos(ang)
    sin = jnp.sin(ang)
    reps = LANES // DH
    cos_t = jnp.tile(jnp.concatenate([cos, cos], axis=1), (1, reps))
    sin_t = jnp.tile(jnp.concatenate([-sin, sin], axis=1), (1, reps))
    return cos_t, sin_t


def kernel(x_prompt, x_sample, cache_k, cache_v, page_table, norm_pre, w_in, lam_q1, lam_k1,
           lam_q2, lam_k2, sub_g, w_pa, ln_g, ln_b, w_s, b_s, w_pb, w_o, norm_post):
    b, s, _ = x_prompt.shape
    db, t, _ = x_sample.shape
    assert b == 1 and t == 1 and norm_pre.shape[0] == 1
    n_past = page_table.shape[1] * cache_k.shape[2]

    w_in_bf = w_in[0].astype(BF16)
    wpa_bf = w_pa[0].astype(BF16)
    wpb_bf = w_pb[0].astype(BF16)
    wo_bf = w_o[0].astype(BF16)
    lams = (lam_q1, lam_k1, lam_q2, lam_k2)

    xp = x_prompt.reshape(s, D_MODEL)
    cos_p, sin_p = _rope_tables(jnp.arange(s, dtype=jnp.int32))
    z_p, k_p, v_p = _inproj(xp, norm_pre, w_in_bf, cos_p, sin_p, tm=512, z_dtype=BF16)
    a_p = _prompt_attn(z_p, *lams, sub_g, tq=256)
    bias_p = jnp.repeat(jnp.transpose(b_s[0]), SG_GROUP, axis=1)
    b_p = _sgate_prompt(z_p, ln_g, ln_b, w_s[0], bias_p, rows=512)
    y_p = _merge(xp, a_p, b_p, z_p, wpa_bf, wpb_bf, wo_bf, norm_post, tm=256)

    xs = x_sample.reshape(db, D_MODEL)
    cos_s, sin_s = _rope_tables(jnp.full((db,), n_past, dtype=jnp.int32))
    z_s, k_s, v_s = _inproj(xs, norm_pre, w_in_bf, cos_s, sin_s, tm=db, z_dtype=F32)
    a_s = _sample_attn(page_table, z_s[:, :D_ATTN], k_s, v_s, z_s[:, 3 * D_ATTN:4 * D_ATTN],
                       cache_k[0], cache_v[0], *lams, sub_g, pages=8)
    w0 = jnp.repeat(w_s[0, :, 0, 0], SG_GROUP)[None, :]
    b0 = jnp.repeat(b_s[0, :, 0], SG_GROUP)[None, :]
    b_sm, vn_s = _sgate_sample(z_s, ln_g, ln_b, w0, b0)
    y_s = _merge(xs, a_s, b_sm, z_s, wpa_bf, wpb_bf, wo_bf, norm_post, tm=db)

    return (y_p.reshape(b, s, D_MODEL),
            y_s.reshape(db, t, D_MODEL),
            k_p.reshape(1, b, s, N_HEADS, 2, DH),
            v_p.reshape(1, b, s, N_HEADS, DV),
            k_s.reshape(1, db, t, N_HEADS, 2, DH),
            v_s.reshape(1, db, t, N_HEADS, DV),
            vn_s.reshape(1, db, t, D_SG))
```

```python
import functools
import math

import jax
import jax.numpy as jnp
from jax import lax
from jax.experimental import pallas as pl
from jax.experimental.pallas import tpu as pltpu

F32 = jnp.float32
BF16 = jnp.bfloat16

D_MODEL = 2048
D_ATTN = D_MODEL // 2
N_HEADS = 8
DH = D_ATTN // (2 * N_HEADS)
DV = 2 * DH
D_SG = D_MODEL // 2
N_GROUPS = 8
SG_GROUP = D_SG // N_GROUPS
CHUNK = 128
ROPE_THETA = 10000.0
EPS = 1e-6
NEG = -1e30
D_IN = 4 * D_ATTN + 3 * D_SG + 2 * D_MODEL
COL_BLOCK = 1024
N_COL_BLOCKS = D_IN // COL_BLOCK
LANES = 128
Q_SCALE = DH ** -0.5 * math.log2(math.e)
LAM_INIT = 0.8 - 0.6 * math.exp(-0.3 * 0)
MIB = 1024 * 1024


def _silu(x):
    return x * jax.nn.sigmoid(x)


def _lam(lq1, lk1, lq2, lk2):
    a = jnp.sum(lq1[...] * lk1[...], axis=-1, keepdims=True)
    b = jnp.sum(lq2[...] * lk2[...], axis=-1, keepdims=True)
    return jnp.exp(a) - jnp.exp(b) + LAM_INIT


def _inproj_kernel(x_ref, g_ref, w_ref, cos_ref, sin_ref, z_ref, k_ref, v_ref, xn_ref):
    j = pl.program_id(1)
    tm = x_ref.shape[0]

    @pl.when(j == 0)
    def _():
        x = x_ref[...]
        ms = jnp.mean(x * x, axis=-1, keepdims=True)
        xn_ref[...] = (x * lax.rsqrt(ms + EPS) * g_ref[...]).astype(BF16)

    acc = jnp.dot(xn_ref[...], w_ref[...], preferred_element_type=F32)

    def rope(a):
        cos = cos_ref[...]
        sin = sin_ref[...]
        lane = lax.broadcasted_iota(jnp.int32, (tm, LANES), 1)
        first_half = (lane % DH) < (DH // 2)
        outs = []
        for c in range(COL_BLOCK // LANES):
            blk = a[:, c * LANES:(c + 1) * LANES]
            partner = jnp.where(first_half,
                                pltpu.roll(blk, LANES - DH // 2, 1),
                                pltpu.roll(blk, DH // 2, 1))
            outs.append(blk * cos + partner * sin)
        return jnp.concatenate(outs, axis=1)

    @pl.when(j == 0)
    def _():
        z_ref[...] = (rope(acc) * Q_SCALE).astype(z_ref.dtype)

    @pl.when(j == 1)
    def _():
        r = rope(acc)
        z_ref[...] = r.astype(z_ref.dtype)
        k_ref[...] = r

    @pl.when(j == 2)
    def _():
        z_ref[...] = acc.astype(z_ref.dtype)
        v_ref[...] = acc

    @pl.when(j > 2)
    def _():
        z_ref[...] = acc.astype(z_ref.dtype)


def _inproj(x, g, w_bf, cos_t, sin_t, *, tm, z_dtype):
    m = x.shape[0]
    return pl.pallas_call(
        _inproj_kernel,
        grid=(m // tm, N_COL_BLOCKS),
        in_specs=[
            pl.BlockSpec((tm, D_MODEL), lambda i, j: (i, 0), pipeline_mode=pl.Buffered(1)),
            pl.BlockSpec((1, D_MODEL), lambda i, j: (0, 0)),
            pl.BlockSpec((D_MODEL, COL_BLOCK), lambda i, j: (0, j)),
            pl.BlockSpec((tm, LANES), lambda i, j: (i, 0)),
            pl.BlockSpec((tm, LANES), lambda i, j: (i, 0)),
        ],
        out_specs=[
            pl.BlockSpec((tm, COL_BLOCK), lambda i, j: (i, j)),
            pl.BlockSpec((tm, COL_BLOCK), lambda i, j: (i, 0), pipeline_mode=pl.Buffered(1)),
            pl.BlockSpec((tm, COL_BLOCK), lambda i, j: (i, 0), pipeline_mode=pl.Buffered(1)),
        ],
        out_shape=[
            jax.ShapeDtypeStruct((m, D_IN), z_dtype),
            jax.ShapeDtypeStruct((m, D_ATTN), F32),
            jax.ShapeDtypeStruct((m, D_ATTN), F32),
        ],
        scratch_shapes=[pltpu.VMEM((tm, D_MODEL), BF16)],
        compiler_params=pltpu.CompilerParams(
            dimension_semantics=("parallel", "arbitrary"),
            vmem_limit_bytes=56 * MIB),
        name="inproj",
    )(x, g, w_bf, cos_t, sin_t)


def _prompt_attn_kernel(q_ref, k_ref, v_ref, ga_ref, lq1, lk1, lq2, lk2, subg_ref, o_ref,
                        vt_ref, qq_ref, sa_ref, sb_ref, m_ref, l_ref, acc_ref, *, tq, tk):
    qi = pl.program_id(1)
    n_kv = k_ref.shape[0] // tk

    @pl.when(qi == 0)
    def _():
        def transpose_block(kb, carry):
            start = pl.multiple_of(kb * tk, tk)
            vt_ref[kb] = v_ref[pl.ds(start, tk), :].astype(F32).T.astype(BF16)
            return carry
        lax.fori_loop(0, n_kv, transpose_block, 0)

    q_t = q_ref[...].astype(F32).T
    feat = lax.broadcasted_iota(jnp.int32, (LANES, tq), 0)
    qq_ref[:, :tq] = jnp.where(feat < DH, q_t, 0.0).astype(BF16)
    qq_ref[:, tq:] = jnp.where(feat >= DH, q_t, 0.0).astype(BF16)

    m_ref[...] = jnp.full(m_ref.shape, NEG, F32)
    l_ref[...] = jnp.zeros(l_ref.shape, F32)
    acc_ref[...] = jnp.zeros(acc_ref.shape, F32)

    def scores(kb, dst):
        start = pl.multiple_of(kb * tk, tk)
        dst[...] = jnp.dot(k_ref[pl.ds(start, tk), :], qq_ref[...],
                           preferred_element_type=F32)

    def consume(src, kb, masked):
        s_t = src[...]
        if masked:
            key = kb * tk + lax.broadcasted_iota(jnp.int32, (tk, 2 * tq), 0)
            qry = lax.broadcasted_iota(jnp.int32, (tk, 2 * tq), 1)
            qry = qi * tq + jnp.where(qry >= tq, qry - tq, qry)
            s_t = jnp.where(key <= qry, s_t, NEG)
        m_old = m_ref[...]
        m_new = jnp.maximum(m_old, jnp.max(s_t, axis=0, keepdims=True))
        alpha = jnp.exp2(m_old - m_new)
        p_t = jnp.exp2(s_t - m_new)
        l_ref[...] = alpha * l_ref[...] + jnp.sum(p_t, axis=0, keepdims=True)
        acc_ref[...] = alpha * acc_ref[...] + jnp.dot(vt_ref[kb], p_t.astype(BF16),
                                                      preferred_element_type=F32)
        m_ref[...] = m_new

    assert tq == 2 * tk
    scores(0, sa_ref)

    def pair(p, carry):
        scores(2 * p + 1, sb_ref)
        consume(sa_ref, 2 * p, False)
        scores(2 * p + 2, sa_ref)
        consume(sb_ref, 2 * p + 1, False)
        return carry

    lax.fori_loop(0, qi, pair, 0)
    scores(2 * qi + 1, sb_ref)
    consume(sa_ref, 2 * qi, True)
    consume(sb_ref, 2 * qi + 1, True)

    lam = _lam(lq1, lk1, lq2, lk2)
    o_all = acc_ref[...] / l_ref[...]
    o = (o_all[:, :tq] - lam * o_all[:, tq:]).T
    ms = jnp.mean(o * o, axis=-1, keepdims=True)
    on = o * lax.rsqrt(ms + EPS) * subg_ref[...] * (1.0 - LAM_INIT)
    ga = ga_ref[...].astype(F32)
    o_ref[...] = (on * _silu(ga)).astype(o_ref.dtype)


def _prompt_attn(z, lq1, lk1, lq2, lk2, subg, *, tq):
    s = z.shape[0]
    tk = tq // 2
    small = lambda n: pl.BlockSpec((1, n), lambda h, i: (0, 0))
    return pl.pallas_call(
        functools.partial(_prompt_attn_kernel, tq=tq, tk=tk),
        grid=(N_HEADS, s // tq),
        in_specs=[
            pl.BlockSpec((tq, LANES), lambda h, i: (i, h)),
            pl.BlockSpec((s, LANES), lambda h, i: (0, N_HEADS + h)),
            pl.BlockSpec((s, LANES), lambda h, i: (0, 2 * N_HEADS + h)),
            pl.BlockSpec((tq, LANES), lambda h, i: (i, 3 * N_HEADS + h)),
            small(DH), small(DH), small(DH), small(DH), small(DV),
        ],
        out_specs=pl.BlockSpec((tq, LANES), lambda h, i: (i, h)),
        out_shape=jax.ShapeDtypeStruct((s, D_ATTN), BF16),
        scratch_shapes=[pltpu.VMEM((s // tk, DV, tk), BF16),
                        pltpu.VMEM((LANES, 2 * tq), BF16),
                        pltpu.VMEM((tk, 2 * tq), F32), pltpu.VMEM((tk, 2 * tq), F32),
                        pltpu.VMEM((1, 2 * tq), F32), pltpu.VMEM((1, 2 * tq), F32),
                        pltpu.VMEM((DV, 2 * tq), F32)],
        compiler_params=pltpu.CompilerParams(
            dimension_semantics=("arbitrary", "arbitrary"),
            vmem_limit_bytes=48 * MIB),
        name="prompt_attn",
    )(z, z, z, z, lq1, lk1, lq2, lk2, subg)


def _sample_attn_kernel(pt_ref, qrow_ref, qcol_ref, kn_ref, vn_ref, ga_ref, lq1, lk1, lq2, lk2,
                        subg_ref, expand_ref, *rest, pages, n_chunks):
    del pt_ref
    k_pages = rest[:pages]
    v_pages = rest[pages:2 * pages]
    o_ref = rest[2 * pages]
    sc_ref, p_ref, pn_ref, acc_ref = rest[2 * pages + 1:]
    c = pl.program_id(1)
    n_maps = 2 * N_HEADS

    @pl.when(c < n_chunks)
    def _():
        qcol = qcol_ref[0]
        for j, kp in enumerate(k_pages):
            prod = (kp[...] * qcol).reshape(N_HEADS, 2 * DH, CHUNK)
            lanes = slice(j * CHUNK, (j + 1) * CHUNK)
            sc_ref[c, :N_HEADS, lanes] = jnp.sum(prod[:, :DH, :], axis=1)
            sc_ref[c, N_HEADS:, lanes] = jnp.sum(prod[:, DH:, :], axis=1)

    @pl.when(c == n_chunks - 1)
    def _():
        s = sc_ref[...]
        row = lax.broadcasted_iota(jnp.int32, (n_maps, D_ATTN), 0)
        seg = lax.broadcasted_iota(jnp.int32, (n_maps, D_ATTN), 1) // DH
        want = jnp.where(row < N_HEADS, 2 * row, 2 * (row - N_HEADS) + 1)
        qk_new = jnp.broadcast_to(qrow_ref[0] * kn_ref[0], (n_maps, D_ATTN))
        s_new = jnp.sum(jnp.where(seg == want, qk_new, 0.0), axis=-1, keepdims=True)
        m = jnp.max(jnp.max(s, axis=-1, keepdims=True), axis=0)
        m = jnp.maximum(m, s_new)
        p = jnp.exp2(s - m[None])
        pn = jnp.exp2(s_new - m)
        l = jnp.sum(jnp.sum(p, axis=-1, keepdims=True), axis=0) + pn
        inv_l = 1.0 / l
        p_ref[...] = (p * inv_l[None]).astype(BF16)
        pn_ref[...] = pn * inv_l
        acc_ref[...] = jnp.zeros(acc_ref.shape, F32)

    @pl.when(c >= n_chunks)
    def _():
        cv = c - n_chunks
        pstack = jnp.concatenate(
            [p_ref[cv, :, j * CHUNK:(j + 1) * CHUNK] for j in range(pages)], axis=0)
        pexp = jnp.dot(pstack, expand_ref[...], preferred_element_type=F32)
        row = lax.broadcasted_iota(jnp.int32, pexp.shape, 0)
        col = lax.broadcasted_iota(jnp.int32, pexp.shape, 1)
        pexp = jnp.where(row % N_HEADS == col % N_HEADS, pexp, 0.0).astype(BF16)
        acc = acc_ref[...]
        for j, vp in enumerate(v_pages):
            acc += jnp.dot(pexp[j * n_maps:(j + 1) * n_maps, :], vp[...].astype(BF16),
                           preferred_element_type=F32)
        acc_ref[...] = acc

    @pl.when(c == 2 * n_chunks - 1)
    def _():
        lam = _lam(lq1, lk1, lq2, lk2)
        v_new = vn_ref[0]
        full = acc_ref[...] + pn_ref[...] * jnp.concatenate([v_new, v_new], axis=0)
        o = full[:N_HEADS, :] - lam * full[N_HEADS:, :]
        ms = jnp.mean(o * o, axis=-1, keepdims=True)
        on = o * lax.rsqrt(ms + EPS) * subg_ref[...] * (1.0 - LAM_INIT)
        o_ref[0] = (on * _silu(ga_ref[0])).astype(o_ref.dtype)


def _sample_attn(page_table, q_s, k_s, v_s, ga_s, cache_k, cache_v, lq1, lk1, lq2, lk2, subg,
                 *, pages):
    db, n_pages = page_table.shape
    n_chunks = n_pages // pages
    span = pages * CHUNK
    n_pool = cache_k.shape[0]
    ck = jnp.transpose(cache_k, (0, 2, 3, 4, 1)).reshape(n_pool, D_ATTN, CHUNK)
    cv = cache_v.reshape(n_pool, CHUNK * N_HEADS, DV)
    pt = page_table.reshape(-1)
    qrow = q_s.reshape(db, 1, D_ATTN)
    qcol = jnp.broadcast_to(q_s[:, :, None], (db, D_ATTN, LANES))
    knew = k_s.reshape(db, 1, D_ATTN)
    vnew = v_s.reshape(db, N_HEADS, DV)
    ga = ga_s.reshape(db, N_HEADS, DV)
    expand = (jnp.arange(CHUNK * N_HEADS)[None, :] // N_HEADS
              == jnp.arange(CHUNK)[:, None]).astype(BF16)

    per_b = lambda shp: pl.BlockSpec((1,) + shp, lambda b, c, pt_: (b, 0, 0))

    def kspec(j):
        return pl.BlockSpec(
            (None, D_ATTN, CHUNK),
            lambda b, c, pt_: (pt_[b * n_pages + jnp.minimum(c, n_chunks - 1) * pages + j], 0, 0))

    def vspec(j):
        return pl.BlockSpec(
            (None, CHUNK * N_HEADS, DV),
            lambda b, c, pt_: (pt_[b * n_pages + jnp.maximum(c - n_chunks, 0) * pages + j], 0, 0))

    small = lambda n: pl.BlockSpec((1, n), lambda b, c, pt_: (0, 0))
    grid_spec = pltpu.PrefetchScalarGridSpec(
        num_scalar_prefetch=1,
        grid=(db, 2 * n_chunks),
        in_specs=[per_b((1, D_ATTN)), per_b((D_ATTN, LANES)), per_b((1, D_ATTN)),
                  per_b((N_HEADS, DV)), per_b((N_HEADS, DV)),
                  small(DH), small(DH), small(DH), small(DH), small(DV),
                  pl.BlockSpec((CHUNK, CHUNK * N_HEADS), lambda b, c, pt_: (0, 0))]
                 + [kspec(j) for j in range(pages)] + [vspec(j) for j in range(pages)],
        out_specs=pl.BlockSpec((1, N_HEADS, DV), lambda b, c, pt_: (b, 0, 0)),
        scratch_shapes=[pltpu.VMEM((n_chunks, 2 * N_HEADS, span), F32),
                        pltpu.VMEM((n_chunks, 2 * N_HEADS, span), BF16),
                        pltpu.VMEM((2 * N_HEADS, 1), F32),
                        pltpu.VMEM((2 * N_HEADS, DV), F32)],
    )
    out = pl.pallas_call(
        functools.partial(_sample_attn_kernel, pages=pages, n_chunks=n_chunks),
        grid_spec=grid_spec,
        out_shape=jax.ShapeDtypeStruct((db, N_HEADS, DV), BF16),
        compiler_params=pltpu.CompilerParams(
            dimension_semantics=("parallel", "arbitrary"),
            vmem_limit_bytes=48 * MIB),
        name="sample_attn",
    )(pt, qrow, qcol, knew, vnew, ga, lq1, lk1, lq2, lk2, subg, expand,
      *([ck] * pages), *([cv] * pages))
    return out.reshape(db, D_ATTN)


def _layernorm(x, g, b):
    mu = jnp.mean(x, axis=-1, keepdims=True)
    xc = x - mu
    var = jnp.mean(xc * xc, axis=-1, keepdims=True)
    return xc * lax.rsqrt(var + EPS) * g + b


def _sgate_prompt_kernel(u_ref, vsg_ref, gb_ref, lng_ref, lnb_ref, ws_ref, bias_ref, o_ref):
    rows = u_ref.shape[0]
    n_c = rows // CHUNK
    vn = _layernorm(vsg_ref[...].astype(F32), lng_ref[...], lnb_ref[...]).astype(BF16)
    t = lax.broadcasted_iota(jnp.int32, (CHUNK, CHUNK), 0)
    s = lax.broadcasted_iota(jnp.int32, (CHUNK, CHUNK), 1)
    causal = s <= t
    for g in range(N_GROUPS):
        cols = slice(g * SG_GROUP, (g + 1) * SG_GROUP)
        wm = jnp.where(causal, ws_ref[g], 0.0).astype(BF16)
        rhs = jnp.concatenate([vn[c * CHUNK:(c + 1) * CHUNK, cols] for c in range(n_c)], axis=1)
        sp = jnp.dot(wm, rhs, preferred_element_type=F32)
        for c in range(n_c):
            rws = slice(c * CHUNK, (c + 1) * CHUNK)
            sp_c = sp[:, c * SG_GROUP:(c + 1) * SG_GROUP] + bias_ref[:, cols]
            gb = gb_ref[rws, cols].astype(F32)
            o_ref[rws, cols] = (u_ref[rws, cols].astype(F32) * sp_c * _silu(gb)).astype(o_ref.dtype)


def _sgate_prompt(z, ln_g, ln_b, w_s, bias, *, rows):
    s = z.shape[0]
    const2 = lambda shp: pl.BlockSpec(shp, lambda i: (0, 0))
    return pl.pallas_call(
        _sgate_prompt_kernel,
        grid=(s // rows,),
        in_specs=[
            pl.BlockSpec((rows, D_SG), lambda i: (i, 4)),
            pl.BlockSpec((rows, D_SG), lambda i: (i, 5)),
            pl.BlockSpec((rows, D_SG), lambda i: (i, 6)),
            const2((1, D_SG)), const2((1, D_SG)),
            pl.BlockSpec((N_GROUPS, CHUNK, CHUNK), lambda i: (0, 0, 0)),
            const2((CHUNK, D_SG)),
        ],
        out_specs=pl.BlockSpec((rows, D_SG), lambda i: (i, 0)),
        out_shape=jax.ShapeDtypeStruct((s, D_SG), BF16),
        compiler_params=pltpu.CompilerParams(dimension_semantics=("parallel",),
                                             vmem_limit_bytes=48 * MIB),
        name="sgate_prompt",
    )(z, z, z, ln_g, ln_b, w_s, bias)


def _sgate_sample_kernel(u_ref, vsg_ref, gb_ref, lng_ref, lnb_ref, w0_ref, b0_ref, o_ref, vn_ref):
    vn = _layernorm(vsg_ref[...], lng_ref[...], lnb_ref[...])
    vn_ref[...] = vn
    sp = w0_ref[...] * vn + b0_ref[...]
    o_ref[...] = (u_ref[...] * sp * _silu(gb_ref[...])).astype(o_ref.dtype)


def _sgate_sample(zs, ln_g, ln_b, w0, b0):
    db = zs.shape[0]
    const2 = lambda shp: pl.BlockSpec(shp, lambda i: (0, 0))
    return pl.pallas_call(
        _sgate_sample_kernel,
        grid=(1,),
        in_specs=[
            pl.BlockSpec((db, D_SG), lambda i: (0, 4)),
            pl.BlockSpec((db, D_SG), lambda i: (0, 5)),
            pl.BlockSpec((db, D_SG), lambda i: (0, 6)),
            const2((1, D_SG)), const2((1, D_SG)), const2((1, D_SG)), const2((1, D_SG)),
        ],
        out_specs=[pl.BlockSpec((db, D_SG), lambda i: (0, 0)),
                   pl.BlockSpec((db, D_SG), lambda i: (0, 0))],
        out_shape=[jax.ShapeDtypeStruct((db, D_SG), BF16),
                   jax.ShapeDtypeStruct((db, D_SG), F32)],
        name="sgate_sample",
    )(zs, zs, zs, ln_g, ln_b, w0, b0)


def _merge_kernel(x_ref, a_ref, b_ref, ga0_ref, ga1_ref, gb0_ref, gb1_ref,
                  wpa_ref, wpb_ref, wo_ref, np_ref, o_ref):
    ya = jnp.dot(a_ref[...], wpa_ref[...], preferred_element_type=F32)
    yb = jnp.dot(b_ref[...], wpb_ref[...], preferred_element_type=F32)
    half = D_MODEL // 2
    sig = lambda r: jax.nn.sigmoid(r[...].astype(F32))
    m_lo = sig(ga0_ref) * ya[:, :half] + sig(gb0_ref) * yb[:, :half]
    m_hi = sig(ga1_ref) * ya[:, half:] + sig(gb1_ref) * yb[:, half:]
    m = jnp.concatenate([m_lo, m_hi], axis=1).astype(BF16)
    mo = jnp.dot(m, wo_ref[...], preferred_element_type=F32)
    ms = jnp.mean(mo * mo, axis=-1, keepdims=True)
    o_ref[...] = x_ref[...] + mo * lax.rsqrt(ms + EPS) * np_ref[...]


def _merge(x, a_in, b_in, z, wpa_bf, wpb_bf, wo_bf, norm_post, *, tm):
    m = x.shape[0]
    resident = lambda shp: pl.BlockSpec(shp, lambda i: (0, 0), pipeline_mode=pl.Buffered(1))
    zcol = lambda jcol: pl.BlockSpec((tm, COL_BLOCK), lambda i: (i, jcol))
    return pl.pallas_call(
        _merge_kernel,
        grid=(m // tm,),
        in_specs=[
            pl.BlockSpec((tm, D_MODEL), lambda i: (i, 0)),
            pl.BlockSpec((tm, D_ATTN), lambda i: (i, 0)),
            pl.BlockSpec((tm, D_SG), lambda i: (i, 0)),
            zcol(7), zcol(8), zcol(9), zcol(10),
            resident((D_ATTN, D_MODEL)), resident((D_SG, D_MODEL)), resident((D_MODEL, D_MODEL)),
            resident((1, D_MODEL)),
        ],
        out_specs=pl.BlockSpec((tm, D_MODEL), lambda i: (i, 0)),
        out_shape=jax.ShapeDtypeStruct((m, D_MODEL), F32),
        compiler_params=pltpu.CompilerParams(dimension_semantics=("parallel",),
                                             vmem_limit_bytes=56 * MIB),
        name="merge",
    )(x, a_in, b_in, z, z, z, z, wpa_bf, wpb_bf, wo_bf, norm_post)


def _rope_tables(pos):
    half = DH // 2
    inv = ROPE_THETA ** (-(jnp.arange(half, dtype=F32) * 2.0 / DH))
    ang = pos.astype(F32)[:, None] * inv[None, :]
    cos = jnp.cos(ang)
    sin = jnp.sin(ang)
    reps = LANES // DH
    cos_t = jnp.tile(jnp.concatenate([cos, cos], axis=1), (1, reps))
    sin_t = jnp.tile(jnp.concatenate([-sin, sin], axis=1), (1, reps))
    return cos_t, sin_t


def kernel(x_prompt, x_sample, cache_k, cache_v, page_table, norm_pre, w_in, lam_q1, lam_k1,
           lam_q2, lam_k2, sub_g, w_pa, ln_g, ln_b, w_s, b_s, w_pb, w_o, norm_post):
    b, s, _ = x_prompt.shape
    db, t, _ = x_sample.shape
    assert b == 1 and t == 1 and norm_pre.shape[0] == 1
    n_past = page_table.shape[1] * cache_k.shape[2]

    w_in_bf = w_in[0].astype(BF16)
    wpa_bf = w_pa[0].astype(BF16)
    wpb_bf = w_pb[0].astype(BF16)
    wo_bf = w_o[0].astype(BF16)
    lams = (lam_q1, lam_k1, lam_q2, lam_k2)

    xp = x_prompt.reshape(s, D_MODEL)
    cos_p, sin_p = _rope_tables(jnp.arange(s, dtype=jnp.int32))
    z_p, k_p, v_p = _inproj(xp, norm_pre, w_in_bf, cos_p, sin_p, tm=1024, z_dtype=BF16)
    a_p = _prompt_attn(z_p, *lams, sub_g, tq=512)
    bias_p = jnp.repeat(jnp.transpose(b_s[0]), SG_GROUP, axis=1)
    b_p = _sgate_prompt(z_p, ln_g, ln_b, w_s[0], bias_p, rows=512)
    y_p = _merge(xp, a_p, b_p, z_p, wpa_bf, wpb_bf, wo_bf, norm_post, tm=256)

    xs = x_sample.reshape(db, D_MODEL)
    cos_s, sin_s = _rope_tables(jnp.full((db,), n_past, dtype=jnp.int32))
    z_s, k_s, v_s = _inproj(xs, norm_pre, w_in_bf, cos_s, sin_s, tm=db, z_dtype=F32)
    a_s = _sample_attn(page_table, z_s[:, :D_ATTN], k_s, v_s, z_s[:, 3 * D_ATTN:4 * D_ATTN],
                       cache_k[0], cache_v[0], *lams, sub_g, pages=8)
    w0 = jnp.repeat(w_s[0, :, 0, 0], SG_GROUP)[None, :]
    b0 = jnp.repeat(b_s[0, :, 0], SG_GROUP)[None, :]
    b_sm, vn_s = _sgate_sample(z_s, ln_g, ln_b, w0, b0)
    y_s = _merge(xs, a_s, b_sm, z_s, wpa_bf, wpb_bf, wo_bf, norm_post, tm=db)

    return (y_p.reshape(b, s, D_MODEL),
            y_s.reshape(db, t, D_MODEL),
            k_p.reshape(1, b, s, N_HEADS, 2, DH),
            v_p.reshape(1, b, s, N_HEADS, DV),
            k_s.reshape(1, db, t, N_HEADS, 2, DH),
            v_s.reshape(1, db, t, N_HEADS, DV),
            vn_s.reshape(1, db, t, D_SG))
```

```python
import functools
import math

import jax
import jax.numpy as jnp
from jax import lax
from jax.experimental import pallas as pl
from jax.experimental.pallas import tpu as pltpu

F32 = jnp.float32
BF16 = jnp.bfloat16

D_MODEL = 2048
D_ATTN = D_MODEL // 2
N_HEADS = 8
DH = D_ATTN // (2 * N_HEADS)
DV = 2 * DH
D_SG = D_MODEL // 2
N_GROUPS = 8
SG_GROUP = D_SG // N_GROUPS
CHUNK = 128
ROPE_THETA = 10000.0
EPS = 1e-6
NEG = -1e30
D_IN = 4 * D_ATTN + 3 * D_SG + 2 * D_MODEL
COL_BLOCK = 1024
N_COL_BLOCKS = D_IN // COL_BLOCK
LANES = 128
Q_SCALE = DH ** -0.5 * math.log2(math.e)
LAM_INIT = 0.8 - 0.6 * math.exp(-0.3 * 0)
MIB = 1024 * 1024
ONES_ROWS = 16


def _silu(x):
    return x * jax.nn.sigmoid(x)


def _lam(lq1, lk1, lq2, lk2):
    a = jnp.sum(lq1[...] * lk1[...], axis=-1, keepdims=True)
    b = jnp.sum(lq2[...] * lk2[...], axis=-1, keepdims=True)
    return jnp.exp(a) - jnp.exp(b) + LAM_INIT


def _inproj_kernel(x_ref, g_ref, w_ref, cos_ref, sin_ref, z_ref, k_ref, v_ref, *rest):
    xn_ref = rest[-1]
    j = pl.program_id(1)
    tm = x_ref.shape[0]

    @pl.when(j == 0)
    def _():
        x = x_ref[...]
        ms = jnp.mean(x * x, axis=-1, keepdims=True)
        xn_ref[...] = (x * lax.rsqrt(ms + EPS) * g_ref[...]).astype(BF16)

    w = w_ref[...].astype(BF16)
    if len(rest) == 2:
        rest[0][...] = w
    acc = jnp.dot(xn_ref[...], w, preferred_element_type=F32)

    def rope(a):
        cos = cos_ref[...]
        sin = sin_ref[...]
        lane = lax.broadcasted_iota(jnp.int32, (tm, LANES), 1)
        first_half = (lane % DH) < (DH // 2)
        outs = []
        for c in range(COL_BLOCK // LANES):
            blk = a[:, c * LANES:(c + 1) * LANES]
            partner = jnp.where(first_half,
                                pltpu.roll(blk, LANES - DH // 2, 1),
                                pltpu.roll(blk, DH // 2, 1))
            outs.append(blk * cos + partner * sin)
        return jnp.concatenate(outs, axis=1)

    @pl.when(j == 0)
    def _():
        z_ref[...] = (rope(acc) * Q_SCALE).astype(z_ref.dtype)

    @pl.when(j == 1)
    def _():
        r = rope(acc)
        z_ref[...] = r.astype(z_ref.dtype)
        k_ref[...] = r

    @pl.when(j == 2)
    def _():
        z_ref[...] = acc.astype(z_ref.dtype)
        v_ref[...] = acc

    @pl.when(j > 2)
    def _():
        z_ref[...] = acc.astype(z_ref.dtype)


def _inproj(x, g, w_bf, cos_t, sin_t, *, tm, z_dtype):
    m = x.shape[0]
    emit_w = w_bf.dtype != BF16
    assert not emit_w or m == tm
    w_out_spec = [pl.BlockSpec((D_MODEL, COL_BLOCK), lambda i, j: (0, j))] if emit_w else []
    w_out_shape = [jax.ShapeDtypeStruct((D_MODEL, D_IN), BF16)] if emit_w else []
    return pl.pallas_call(
        _inproj_kernel,
        grid=(m // tm, N_COL_BLOCKS),
        in_specs=[
            pl.BlockSpec((tm, D_MODEL), lambda i, j: (i, 0), pipeline_mode=pl.Buffered(1)),
            pl.BlockSpec((1, D_MODEL), lambda i, j: (0, 0)),
            pl.BlockSpec((D_MODEL, COL_BLOCK), lambda i, j: (0, j)),
            pl.BlockSpec((tm, LANES), lambda i, j: (i, 0)),
            pl.BlockSpec((tm, LANES), lambda i, j: (i, 0)),
        ],
        out_specs=[
            pl.BlockSpec((tm, COL_BLOCK), lambda i, j: (i, j)),
            pl.BlockSpec((tm, COL_BLOCK), lambda i, j: (i, 0), pipeline_mode=pl.Buffered(1)),
            pl.BlockSpec((tm, COL_BLOCK), lambda i, j: (i, 0), pipeline_mode=pl.Buffered(1)),
        ] + w_out_spec,
        out_shape=[
            jax.ShapeDtypeStruct((m, D_IN), z_dtype),
            jax.ShapeDtypeStruct((m, D_ATTN), F32),
            jax.ShapeDtypeStruct((m, D_ATTN), F32),
        ] + w_out_shape,
        scratch_shapes=[pltpu.VMEM((tm, D_MODEL), BF16)],
        compiler_params=pltpu.CompilerParams(
            dimension_semantics=("parallel", "arbitrary"),
            vmem_limit_bytes=56 * MIB),
        name="inproj",
    )(x, g, w_bf, cos_t, sin_t)


def _prompt_attn_kernel(q_ref, k_ref, v_ref, ga_ref, lq1, lk1, lq2, lk2, subg_ref, o_ref,
                        vt_ref, qq_ref, sa_ref, sb_ref, m_ref, acc_ref, *, tq, tk):
    qi = pl.program_id(1)
    n_kv = k_ref.shape[0] // tk

    @pl.when(qi == 0)
    def _():
        def transpose_block(kb, carry):
            start = pl.multiple_of(kb * tk, tk)
            vt_ref[kb, :DV, :] = v_ref[pl.ds(start, tk), :].astype(F32).T.astype(BF16)
            vt_ref[kb, DV:, :] = jnp.ones((ONES_ROWS, tk), BF16)
            return carry
        lax.fori_loop(0, n_kv, transpose_block, 0)

    q_t = q_ref[...].astype(F32).T
    feat = lax.broadcasted_iota(jnp.int32, (LANES, tq), 0)
    qq_ref[:, :tq] = jnp.where(feat < DH, q_t, 0.0).astype(BF16)
    qq_ref[:, tq:] = jnp.where(feat >= DH, q_t, 0.0).astype(BF16)

    m_ref[...] = jnp.full(m_ref.shape, NEG, F32)
    acc_ref[...] = jnp.zeros(acc_ref.shape, F32)

    def scores(kb, dst):
        start = pl.multiple_of(kb * tk, tk)
        dst[...] = jnp.dot(k_ref[pl.ds(start, tk), :], qq_ref[...],
                           preferred_element_type=F32)

    def consume(src, kb, masked):
        s_t = src[...]
        if masked:
            key = kb * tk + lax.broadcasted_iota(jnp.int32, (tk, 2 * tq), 0)
            qry = lax.broadcasted_iota(jnp.int32, (tk, 2 * tq), 1)
            qry = qi * tq + jnp.where(qry >= tq, qry - tq, qry)
            s_t = jnp.where(key <= qry, s_t, NEG)
        m_old = m_ref[...]
        m_new = jnp.maximum(m_old, jnp.max(s_t, axis=0, keepdims=True))
        alpha = jnp.exp2(m_old - m_new)
        p_t = jnp.exp2(s_t - m_new)
        acc_ref[...] = alpha * acc_ref[...] + jnp.dot(vt_ref[kb], p_t.astype(BF16),
                                                      preferred_element_type=F32)
        m_ref[...] = m_new

    assert tq == 2 * tk
    scores(0, sa_ref)

    def pair(first):
        scores(first + 1, sb_ref)
        consume(sa_ref, first, False)
        scores(first + 2, sa_ref)
        consume(sb_ref, first + 1, False)

    def two_pairs(p, carry):
        pair(4 * p)
        pair(4 * p + 2)
        return carry

    lax.fori_loop(0, qi // 2, two_pairs, 0)

    @pl.when(qi % 2 == 1)
    def _():
        pair(2 * qi - 2)

    scores(2 * qi + 1, sb_ref)
    consume(sa_ref, 2 * qi, True)
    consume(sb_ref, 2 * qi + 1, True)

    lam = _lam(lq1, lk1, lq2, lk2)
    o_all = acc_ref[:DV, :] / acc_ref[DV:DV + 1, :]
    o = (o_all[:, :tq] - lam * o_all[:, tq:]).T
    ms = jnp.mean(o * o, axis=-1, keepdims=True)
    on = o * lax.rsqrt(ms + EPS) * subg_ref[...] * (1.0 - LAM_INIT)
    ga = ga_ref[...].astype(F32)
    o_ref[...] = (on * _silu(ga)).astype(o_ref.dtype)


def _prompt_attn(z, lq1, lk1, lq2, lk2, subg, *, tq):
    s = z.shape[0]
    tk = tq // 2
    small = lambda n: pl.BlockSpec((1, n), lambda h, i: (0, 0))
    return pl.pallas_call(
        functools.partial(_prompt_attn_kernel, tq=tq, tk=tk),
        grid=(N_HEADS, s // tq),
        in_specs=[
            pl.BlockSpec((tq, LANES), lambda h, i: (i, h)),
            pl.BlockSpec((s, LANES), lambda h, i: (0, N_HEADS + h)),
            pl.BlockSpec((s, LANES), lambda h, i: (0, 2 * N_HEADS + h)),
            pl.BlockSpec((tq, LANES), lambda h, i: (i, 3 * N_HEADS + h)),
            small(DH), small(DH), small(DH), small(DH), small(DV),
        ],
        out_specs=pl.BlockSpec((tq, LANES), lambda h, i: (i, h)),
        out_shape=jax.ShapeDtypeStruct((s, D_ATTN), BF16),
        scratch_shapes=[pltpu.VMEM((s // tk, DV + ONES_ROWS, tk), BF16),
                        pltpu.VMEM((LANES, 2 * tq), BF16),
                        pltpu.VMEM((tk, 2 * tq), F32), pltpu.VMEM((tk, 2 * tq), F32),
                        pltpu.VMEM((1, 2 * tq), F32),
                        pltpu.VMEM((DV + ONES_ROWS, 2 * tq), F32)],
        compiler_params=pltpu.CompilerParams(
            dimension_semantics=("arbitrary", "arbitrary"),
            vmem_limit_bytes=48 * MIB),
        name="prompt_attn",
    )(z, z, z, z, lq1, lk1, lq2, lk2, subg)


def _sample_attn_kernel(pt_ref, qrow_ref, qcol_ref, kn_ref, vn_ref, ga_ref, lq1, lk1, lq2, lk2,
                        subg_ref, expand_ref, *rest, pages, n_chunks):
    del pt_ref
    k_pages = rest[:pages]
    v_pages = rest[pages:2 * pages]
    o_ref = rest[2 * pages]
    sc_ref, p_ref, pn_ref, acc_ref = rest[2 * pages + 1:]
    c = pl.program_id(1)
    n_maps = 2 * N_HEADS

    @pl.when(c < n_chunks)
    def _():
        qcol = qcol_ref[0]
        for j, kp in enumerate(k_pages):
            prod = (kp[...] * qcol).reshape(N_HEADS, 2 * DH, CHUNK)
            lanes = slice(j * CHUNK, (j + 1) * CHUNK)
            sc_ref[c, :N_HEADS, lanes] = jnp.sum(prod[:, :DH, :], axis=1)
            sc_ref[c, N_HEADS:, lanes] = jnp.sum(prod[:, DH:, :], axis=1)

    @pl.when(c == n_chunks - 1)
    def _():
        s = sc_ref[...]
        row = lax.broadcasted_iota(jnp.int32, (n_maps, D_ATTN), 0)
        seg = lax.broadcasted_iota(jnp.int32, (n_maps, D_ATTN), 1) // DH
        want = jnp.where(row < N_HEADS, 2 * row, 2 * (row - N_HEADS) + 1)
        qk_new = jnp.broadcast_to(qrow_ref[0] * kn_ref[0], (n_maps, D_ATTN))
        s_new = jnp.sum(jnp.where(seg == want, qk_new, 0.0), axis=-1, keepdims=True)
        m = jnp.max(jnp.max(s, axis=-1, keepdims=True), axis=0)
        m = jnp.maximum(m, s_new)
        p = jnp.exp2(s - m[None])
        pn = jnp.exp2(s_new - m)
        l = jnp.sum(jnp.sum(p, axis=-1, keepdims=True), axis=0) + pn
        inv_l = 1.0 / l
        p_ref[...] = (p * inv_l[None]).astype(BF16)
        pn_ref[...] = pn * inv_l
        acc_ref[...] = jnp.zeros(acc_ref.shape, F32)

    @pl.when(c >= n_chunks)
    def _():
        cv = c - n_chunks
        pstack = jnp.concatenate(
            [p_ref[cv, :, j * CHUNK:(j + 1) * CHUNK] for j in range(pages)], axis=0)
        pexp = jnp.dot(pstack, expand_ref[...], preferred_element_type=F32)
        row = lax.broadcasted_iota(jnp.int32, pexp.shape, 0)
        col = lax.broadcasted_iota(jnp.int32, pexp.shape, 1)
        pexp = jnp.where(row % N_HEADS == col % N_HEADS, pexp, 0.0).astype(BF16)
        acc = acc_ref[...]
        for j, vp in enumerate(v_pages):
            acc += jnp.dot(pexp[j * n_maps:(j + 1) * n_maps, :], vp[...].astype(BF16),
                           preferred_element_type=F32)
        acc_ref[...] = acc

    @pl.when(c == 2 * n_chunks - 1)
    def _():
        lam = _lam(lq1, lk1, lq2, lk2)
        v_new = vn_ref[0]
        full = acc_ref[...] + pn_ref[...] * jnp.concatenate([v_new, v_new], axis=0)
        o = full[:N_HEADS, :] - lam * full[N_HEADS:, :]
        ms = jnp.mean(o * o, axis=-1, keepdims=True)
        on = o * lax.rsqrt(ms + EPS) * subg_ref[...] * (1.0 - LAM_INIT)
        o_ref[0] = (on * _silu(ga_ref[0])).astype(o_ref.dtype)


def _sample_attn(page_table, q_s, k_s, v_s, ga_s, cache_k, cache_v, lq1, lk1, lq2, lk2, subg,
                 *, pages):
    db, n_pages = page_table.shape
    n_chunks = n_pages // pages
    span = pages * CHUNK
    n_pool = cache_k.shape[0]
    ck = jnp.transpose(cache_k, (0, 2, 3, 4, 1)).reshape(n_pool, D_ATTN, CHUNK)
    cv = cache_v.reshape(n_pool, CHUNK * N_HEADS, DV)
    pt = page_table.reshape(-1)
    qrow = q_s.reshape(db, 1, D_ATTN)
    qcol = jnp.broadcast_to(q_s[:, :, None], (db, D_ATTN, LANES))
    knew = k_s.reshape(db, 1, D_ATTN)
    vnew = v_s.reshape(db, N_HEADS, DV)
    ga = ga_s.reshape(db, N_HEADS, DV)
    expand = (jnp.arange(CHUNK * N_HEADS)[None, :] // N_HEADS
              == jnp.arange(CHUNK)[:, None]).astype(BF16)

    per_b = lambda shp: pl.BlockSpec((1,) + shp, lambda b, c, pt_: (b, 0, 0))

    def kspec(j):
        return pl.BlockSpec(
            (None, D_ATTN, CHUNK),
            lambda b, c, pt_: (pt_[b * n_pages + jnp.minimum(c, n_chunks - 1) * pages + j], 0, 0))

    def vspec(j):
        return pl.BlockSpec(
            (None, CHUNK * N_HEADS, DV),
            lambda b, c, pt_: (pt_[b * n_pages + jnp.maximum(c - n_chunks, 0) * pages + j], 0, 0))

    small = lambda n: pl.BlockSpec((1, n), lambda b, c, pt_: (0, 0))
    grid_spec = pltpu.PrefetchScalarGridSpec(
        num_scalar_prefetch=1,
        grid=(db, 2 * n_chunks),
        in_specs=[per_b((1, D_ATTN)), per_b((D_ATTN, LANES)), per_b((1, D_ATTN)),
                  per_b((N_HEADS, DV)), per_b((N_HEADS, DV)),
                  small(DH), small(DH), small(DH), small(DH), small(DV),
                  pl.BlockSpec((CHUNK, CHUNK * N_HEADS), lambda b, c, pt_: (0, 0))]
                 + [kspec(j) for j in range(pages)] + [vspec(j) for j in range(pages)],
        out_specs=pl.BlockSpec((1, N_HEADS, DV), lambda b, c, pt_: (b, 0, 0)),
        scratch_shapes=[pltpu.VMEM((n_chunks, 2 * N_HEADS, span), F32),
                        pltpu.VMEM((n_chunks, 2 * N_HEADS, span), BF16),
                        pltpu.VMEM((2 * N_HEADS, 1), F32),
                        pltpu.VMEM((2 * N_HEADS, DV), F32)],
    )
    out = pl.pallas_call(
        functools.partial(_sample_attn_kernel, pages=pages, n_chunks=n_chunks),
        grid_spec=grid_spec,
        out_shape=jax.ShapeDtypeStruct((db, N_HEADS, DV), BF16),
        compiler_params=pltpu.CompilerParams(
            dimension_semantics=("parallel", "arbitrary"),
            vmem_limit_bytes=48 * MIB),
        name="sample_attn",
    )(pt, qrow, qcol, knew, vnew, ga, lq1, lk1, lq2, lk2, subg, expand,
      *([ck] * pages), *([cv] * pages))
    return out.reshape(db, D_ATTN)


def _layernorm(x, g, b):
    mu = jnp.mean(x, axis=-1, keepdims=True)
    xc = x - mu
    var = jnp.mean(xc * xc, axis=-1, keepdims=True)
    return xc * lax.rsqrt(var + EPS) * g + b


def _sgate_prompt_kernel(u_ref, vsg_ref, gb_ref, lng_ref, lnb_ref, ws_ref, bias_ref, o_ref):
    rows = u_ref.shape[0]
    n_c = rows // CHUNK
    vn = _layernorm(vsg_ref[...].astype(F32), lng_ref[...], lnb_ref[...]).astype(BF16)
    t = lax.broadcasted_iota(jnp.int32, (CHUNK, CHUNK), 0)
    s = lax.broadcasted_iota(jnp.int32, (CHUNK, CHUNK), 1)
    causal = s <= t
    for g in range(N_GROUPS):
        cols = slice(g * SG_GROUP, (g + 1) * SG_GROUP)
        wm = jnp.where(causal, ws_ref[g], 0.0).astype(BF16)
        rhs = jnp.concatenate([vn[c * CHUNK:(c + 1) * CHUNK, cols] for c in range(n_c)], axis=1)
        sp = jnp.dot(wm, rhs, preferred_element_type=F32)
        for c in range(n_c):
            rws = slice(c * CHUNK, (c + 1) * CHUNK)
            sp_c = sp[:, c * SG_GROUP:(c + 1) * SG_GROUP] + bias_ref[:, cols]
            gb = gb_ref[rws, cols].astype(F32)
            o_ref[rws, cols] = (u_ref[rws, cols].astype(F32) * sp_c * _silu(gb)).astype(o_ref.dtype)


def _sgate_prompt(z, ln_g, ln_b, w_s, bias, *, rows):
    s = z.shape[0]
    const2 = lambda shp: pl.BlockSpec(shp, lambda i: (0, 0))
    return pl.pallas_call(
        _sgate_prompt_kernel,
        grid=(s // rows,),
        in_specs=[
            pl.BlockSpec((rows, D_SG), lambda i: (i, 4)),
            pl.BlockSpec((rows, D_SG), lambda i: (i, 5)),
            pl.BlockSpec((rows, D_SG), lambda i: (i, 6)),
            const2((1, D_SG)), const2((1, D_SG)),
            pl.BlockSpec((N_GROUPS, CHUNK, CHUNK), lambda i: (0, 0, 0)),
            const2((CHUNK, D_SG)),
        ],
        out_specs=pl.BlockSpec((rows, D_SG), lambda i: (i, 0)),
        out_shape=jax.ShapeDtypeStruct((s, D_SG), BF16),
        compiler_params=pltpu.CompilerParams(dimension_semantics=("parallel",),
                                             vmem_limit_bytes=48 * MIB),
        name="sgate_prompt",
    )(z, z, z, ln_g, ln_b, w_s, bias)


def _sgate_sample_kernel(u_ref, vsg_ref, gb_ref, lng_ref, lnb_ref, w0_ref, b0_ref, o_ref, vn_ref):
    vn = _layernorm(vsg_ref[...], lng_ref[...], lnb_ref[...])
    vn_ref[...] = vn
    sp = w0_ref[...] * vn + b0_ref[...]
    o_ref[...] = (u_ref[...] * sp * _silu(gb_ref[...])).astype(o_ref.dtype)


def _sgate_sample(zs, ln_g, ln_b, w0, b0):
    db = zs.shape[0]
    const2 = lambda shp: pl.BlockSpec(shp, lambda i: (0, 0))
    return pl.pallas_call(
        _sgate_sample_kernel,
        grid=(1,),
        in_specs=[
            pl.BlockSpec((db, D_SG), lambda i: (0, 4)),
            pl.BlockSpec((db, D_SG), lambda i: (0, 5)),
            pl.BlockSpec((db, D_SG), lambda i: (0, 6)),
            const2((1, D_SG)), const2((1, D_SG)), const2((1, D_SG)), const2((1, D_SG)),
        ],
        out_specs=[pl.BlockSpec((db, D_SG), lambda i: (0, 0)),
                   pl.BlockSpec((db, D_SG), lambda i: (0, 0))],
        out_shape=[jax.ShapeDtypeStruct((db, D_SG), BF16),
                   jax.ShapeDtypeStruct((db, D_SG), F32)],
        name="sgate_sample",
    )(zs, zs, zs, ln_g, ln_b, w0, b0)


def _merge_kernel(x_ref, a_ref, b_ref, ga0_ref, ga1_ref, gb0_ref, gb1_ref,
                  wpa_ref, wpb_ref, wo_ref, np_ref, o_ref):
    ya = jnp.dot(a_ref[...], wpa_ref[...], preferred_element_type=F32)
    yb = jnp.dot(b_ref[...], wpb_ref[...], preferred_element_type=F32)
    half = D_MODEL // 2
    sig = lambda r: jax.nn.sigmoid(r[...].astype(F32))
    m_lo = sig(ga0_ref) * ya[:, :half] + sig(gb0_ref) * yb[:, :half]
    m_hi = sig(ga1_ref) * ya[:, half:] + sig(gb1_ref) * yb[:, half:]
    m = jnp.concatenate([m_lo, m_hi], axis=1).astype(BF16)
    mo = jnp.dot(m, wo_ref[...], preferred_element_type=F32)
    ms = jnp.mean(mo * mo, axis=-1, keepdims=True)
    o_ref[...] = x_ref[...] + mo * lax.rsqrt(ms + EPS) * np_ref[...]


def _merge(x, a_in, b_in, z, wpa_bf, wpb_bf, wo_bf, norm_post, *, tm):
    m = x.shape[0]
    resident = lambda shp: pl.BlockSpec(shp, lambda i: (0, 0), pipeline_mode=pl.Buffered(1))
    zcol = lambda jcol: pl.BlockSpec((tm, COL_BLOCK), lambda i: (i, jcol))
    return pl.pallas_call(
        _merge_kernel,
        grid=(m // tm,),
        in_specs=[
            pl.BlockSpec((tm, D_MODEL), lambda i: (i, 0)),
            pl.BlockSpec((tm, D_ATTN), lambda i: (i, 0)),
            pl.BlockSpec((tm, D_SG), lambda i: (i, 0)),
            zcol(7), zcol(8), zcol(9), zcol(10),
            resident((D_ATTN, D_MODEL)), resident((D_SG, D_MODEL)), resident((D_MODEL, D_MODEL)),
            resident((1, D_MODEL)),
        ],
        out_specs=pl.BlockSpec((tm, D_MODEL), lambda i: (i, 0)),
        out_shape=jax.ShapeDtypeStruct((m, D_MODEL), F32),
        compiler_params=pltpu.CompilerParams(dimension_semantics=("parallel",),
                                             vmem_limit_bytes=56 * MIB),
        name="merge",
    )(x, a_in, b_in, z, z, z, z, wpa_bf, wpb_bf, wo_bf, norm_post)


def _rope_tables(pos):
    half = DH // 2
    inv = ROPE_THETA ** (-(jnp.arange(half, dtype=F32) * 2.0 / DH))
    ang = pos.astype(F32)[:, None] * inv[None, :]
    cos = jnp.cos(ang)
    sin = jnp.sin(ang)
    reps = LANES // DH
    cos_t = jnp.tile(jnp.concatenate([cos, cos], axis=1), (1, reps))
    sin_t = jnp.tile(jnp.concatenate([-sin, sin], axis=1), (1, reps))
    return cos_t, sin_t


def kernel(x_prompt, x_sample, cache_k, cache_v, page_table, norm_pre, w_in, lam_q1, lam_k1,
           lam_q2, lam_k2, sub_g, w_pa, ln_g, ln_b, w_s, b_s, w_pb, w_o, norm_post):
    b, s, _ = x_prompt.shape
    db, t, _ = x_sample.shape
    assert b == 1 and t == 1 and norm_pre.shape[0] == 1
    n_past = page_table.shape[1] * cache_k.shape[2]

    wpa_bf = w_pa[0].astype(BF16)
    wpb_bf = w_pb[0].astype(BF16)
    wo_bf = w_o[0].astype(BF16)
    lams = (lam_q1, lam_k1, lam_q2, lam_k2)

    xs = x_sample.reshape(db, D_MODEL)
    cos_s, sin_s = _rope_tables(jnp.full((db,), n_past, dtype=jnp.int32))
    z_s, k_s, v_s, w_in_bf = _inproj(xs, norm_pre, w_in[0], cos_s, sin_s, tm=db, z_dtype=F32)

    xp = x_prompt.reshape(s, D_MODEL)
    cos_p, sin_p = _rope_tables(jnp.arange(s, dtype=jnp.int32))
    z_p, k_p, v_p = _inproj(xp, norm_pre, w_in_bf, cos_p, sin_p, tm=1024, z_dtype=BF16)
    a_p = _prompt_attn(z_p, *lams, sub_g, tq=512)
    bias_p = jnp.repeat(jnp.transpose(b_s[0]), SG_GROUP, axis=1)
    b_p = _sgate_prompt(z_p, ln_g, ln_b, w_s[0], bias_p, rows=512)
    y_p = _merge(xp, a_p, b_p, z_p, wpa_bf, wpb_bf, wo_bf, norm_post, tm=256)

    a_s = _sample_attn(page_table, z_s[:, :D_ATTN], k_s, v_s, z_s[:, 3 * D_ATTN:4 * D_ATTN],
                       cache_k[0], cache_v[0], *lams, sub_g, pages=16)
    w0 = jnp.repeat(w_s[0, :, 0, 0], SG_GROUP)[None, :]
    b0 = jnp.repeat(b_s[0, :, 0], SG_GROUP)[None, :]
    b_sm, vn_s = _sgate_sample(z_s, ln_g, ln_b, w0, b0)
    y_s = _merge(xs, a_s, b_sm, z_s, wpa_bf, wpb_bf, wo_bf, norm_post, tm=db)

    return (y_p.reshape(b, s, D_MODEL),
            y_s.reshape(db, t, D_MODEL),
            k_p.reshape(1, b, s, N_HEADS, 2, DH),
            v_p.reshape(1, b, s, N_HEADS, DV),
            k_s.reshape(1, db, t, N_HEADS, 2, DH),
            v_s.reshape(1, db, t, N_HEADS, DV),
            vn_s.reshape(1, db, t, D_SG))
```

```python
import functools
import math

import jax
import jax.numpy as jnp
from jax import lax
from jax.experimental import pallas as pl
from jax.experimental.pallas import tpu as pltpu

F32 = jnp.float32
BF16 = jnp.bfloat16

D_MODEL = 2048
D_ATTN = D_MODEL // 2
N_HEADS = 8
DH = D_ATTN // (2 * N_HEADS)
DV = 2 * DH
D_SG = D_MODEL // 2
N_GROUPS = 8
SG_GROUP = D_SG // N_GROUPS
CHUNK = 128
ROPE_THETA = 10000.0
EPS = 1e-6
NEG = -1e30
D_IN = 4 * D_ATTN + 3 * D_SG + 2 * D_MODEL
COL_BLOCK = 1024
N_COL_BLOCKS = D_IN // COL_BLOCK
LANES = 128
Q_SCALE = DH ** -0.5 * math.log2(math.e)
LAM_INIT = 0.8 - 0.6 * math.exp(-0.3 * 0)
MIB = 1024 * 1024
ONES_ROWS = 16
SAMPLE_PAGES = 16


def _silu(x):
    return x * jax.nn.sigmoid(x)


def _lam(lq1, lk1, lq2, lk2):
    a = jnp.sum(lq1[...] * lk1[...], axis=-1, keepdims=True)
    b = jnp.sum(lq2[...] * lk2[...], axis=-1, keepdims=True)
    return jnp.exp(a) - jnp.exp(b) + LAM_INIT


def _inproj_kernel(x_ref, g_ref, w_ref, cos_ref, sin_ref, z_ref, k_ref, v_ref, kh_ref, vh_ref,
                   *rest):
    xn_ref = rest[-1]

    def by_head(dst, a):
        for h in range(N_HEADS):
            dst[h] = a[:, h * LANES:(h + 1) * LANES].astype(dst.dtype)

    j = pl.program_id(1)
    tm = x_ref.shape[0]

    @pl.when(j == 0)
    def _():
        x = x_ref[...]
        ms = jnp.mean(x * x, axis=-1, keepdims=True)
        xn_ref[...] = (x * lax.rsqrt(ms + EPS) * g_ref[...]).astype(BF16)

    w = w_ref[...].astype(BF16)
    if len(rest) == 2:
        rest[0][...] = w
    acc = jnp.dot(xn_ref[...], w, preferred_element_type=F32)

    def rope(a):
        cos = cos_ref[...]
        sin = sin_ref[...]
        lane = lax.broadcasted_iota(jnp.int32, (tm, LANES), 1)
        first_half = (lane % DH) < (DH // 2)
        outs = []
        for c in range(COL_BLOCK // LANES):
            blk = a[:, c * LANES:(c + 1) * LANES]
            partner = jnp.where(first_half,
                                pltpu.roll(blk, LANES - DH // 2, 1),
                                pltpu.roll(blk, DH // 2, 1))
            outs.append(blk * cos + partner * sin)
        return jnp.concatenate(outs, axis=1)

    @pl.when(j == 0)
    def _():
        z_ref[...] = (rope(acc) * Q_SCALE).astype(z_ref.dtype)

    @pl.when(j == 1)
    def _():
        r = rope(acc)
        z_ref[...] = r.astype(z_ref.dtype)
        k_ref[...] = r
        by_head(kh_ref, r)

    @pl.when(j == 2)
    def _():
        z_ref[...] = acc.astype(z_ref.dtype)
        v_ref[...] = acc
        by_head(vh_ref, acc)

    @pl.when(j > 2)
    def _():
        z_ref[...] = acc.astype(z_ref.dtype)


def _inproj(x, g, w_bf, cos_t, sin_t, *, tm, z_dtype):
    m = x.shape[0]
    emit_w = w_bf.dtype != BF16
    assert not emit_w or m == tm
    w_out_spec = [pl.BlockSpec((D_MODEL, COL_BLOCK), lambda i, j: (0, j))] if emit_w else []
    w_out_shape = [jax.ShapeDtypeStruct((D_MODEL, D_IN), BF16)] if emit_w else []
    return pl.pallas_call(
        _inproj_kernel,
        grid=(m // tm, N_COL_BLOCKS),
        in_specs=[
            pl.BlockSpec((tm, D_MODEL), lambda i, j: (i, 0), pipeline_mode=pl.Buffered(1)),
            pl.BlockSpec((1, D_MODEL), lambda i, j: (0, 0)),
            pl.BlockSpec((D_MODEL, COL_BLOCK), lambda i, j: (0, j)),
            pl.BlockSpec((tm, LANES), lambda i, j: (i, 0)),
            pl.BlockSpec((tm, LANES), lambda i, j: (i, 0)),
        ],
        out_specs=[
            pl.BlockSpec((tm, COL_BLOCK), lambda i, j: (i, j)),
            pl.BlockSpec((tm, COL_BLOCK), lambda i, j: (i, 0), pipeline_mode=pl.Buffered(1)),
            pl.BlockSpec((tm, COL_BLOCK), lambda i, j: (i, 0), pipeline_mode=pl.Buffered(1)),
            pl.BlockSpec((N_HEADS, tm, LANES), lambda i, j: (0, i, 0), pipeline_mode=pl.Buffered(1)),
            pl.BlockSpec((N_HEADS, tm, LANES), lambda i, j: (0, i, 0), pipeline_mode=pl.Buffered(1)),
        ] + w_out_spec,
        out_shape=[
            jax.ShapeDtypeStruct((m, D_IN), z_dtype),
            jax.ShapeDtypeStruct((m, D_ATTN), F32),
            jax.ShapeDtypeStruct((m, D_ATTN), F32),
            jax.ShapeDtypeStruct((N_HEADS, m, LANES), BF16),
            jax.ShapeDtypeStruct((N_HEADS, m, LANES), BF16),
        ] + w_out_shape,
        scratch_shapes=[pltpu.VMEM((tm, D_MODEL), BF16)],
        compiler_params=pltpu.CompilerParams(
            dimension_semantics=("parallel", "arbitrary"),
            vmem_limit_bytes=56 * MIB),
        name="inproj",
    )(x, g, w_bf, cos_t, sin_t)


def _prompt_attn_step(qi, q_ref, k_ref, v_ref, ga_ref, lq1, lk1, lq2, lk2, subg_ref, o_ref,
                      vt_ref, qq_ref, sa_ref, sb_ref, m_ref, acc_ref, *, tq, tk):
    n_kv = k_ref.shape[0] // tk

    @pl.when(qi == 0)
    def _():
        def transpose_block(kb, carry):
            start = pl.multiple_of(kb * tk, tk)
            vt_ref[kb, :DV, :] = v_ref[pl.ds(start, tk), :].astype(F32).T.astype(BF16)
            vt_ref[kb, DV:, :] = jnp.ones((ONES_ROWS, tk), BF16)
            return carry
        lax.fori_loop(0, n_kv, transpose_block, 0)

    q_t = q_ref[...].astype(F32).T
    feat = lax.broadcasted_iota(jnp.int32, (LANES, tq), 0)
    qq_ref[:, :tq] = jnp.where(feat < DH, q_t, 0.0).astype(BF16)
    qq_ref[:, tq:] = jnp.where(feat >= DH, q_t, 0.0).astype(BF16)

    m_ref[...] = jnp.full(m_ref.shape, NEG, F32)
    acc_ref[...] = jnp.zeros(acc_ref.shape, F32)

    def scores(kb, dst):
        start = pl.multiple_of(kb * tk, tk)
        dst[...] = jnp.dot(k_ref[pl.ds(start, tk), :], qq_ref[...],
                           preferred_element_type=F32)

    def consume(src, kb, masked):
        s_t = src[...]
        if masked:
            key = kb * tk + lax.broadcasted_iota(jnp.int32, (tk, 2 * tq), 0)
            qry = lax.broadcasted_iota(jnp.int32, (tk, 2 * tq), 1)
            qry = qi * tq + jnp.where(qry >= tq, qry - tq, qry)
            s_t = jnp.where(key <= qry, s_t, NEG)
        m_old = m_ref[...]
        m_new = jnp.maximum(m_old, jnp.max(s_t, axis=0, keepdims=True))
        alpha = jnp.exp2(m_old - m_new)
        p_t = jnp.exp2(s_t - m_new)
        acc_ref[...] = alpha * acc_ref[...] + jnp.dot(vt_ref[kb], p_t.astype(BF16),
                                                      preferred_element_type=F32)
        m_ref[...] = m_new

    assert tq == 2 * tk
    scores(0, sa_ref)

    def pair(first):
        scores(first + 1, sb_ref)
        consume(sa_ref, first, False)
        scores(first + 2, sa_ref)
        consume(sb_ref, first + 1, False)

    def two_pairs(p, carry):
        pair(4 * p)
        pair(4 * p + 2)
        return carry

    lax.fori_loop(0, qi // 2, two_pairs, 0)

    @pl.when(qi % 2 == 1)
    def _():
        pair(2 * qi - 2)

    scores(2 * qi + 1, sb_ref)
    consume(sa_ref, 2 * qi, True)
    consume(sb_ref, 2 * qi + 1, True)

    lam = _lam(lq1, lk1, lq2, lk2)
    o_all = acc_ref[:DV, :] / acc_ref[DV:DV + 1, :]
    o = (o_all[:, :tq] - lam * o_all[:, tq:]).T
    ms = jnp.mean(o * o, axis=-1, keepdims=True)
    on = o * lax.rsqrt(ms + EPS) * subg_ref[...] * (1.0 - LAM_INIT)
    ga = ga_ref[...].astype(F32)
    o_ref[...] = (on * _silu(ga)).astype(o_ref.dtype)


def _prompt_attn_scratch(s, tq, tk):
    return [pltpu.VMEM((s // tk, DV + ONES_ROWS, tk), BF16),
            pltpu.VMEM((LANES, 2 * tq), BF16),
            pltpu.VMEM((tk, 2 * tq), F32), pltpu.VMEM((tk, 2 * tq), F32),
            pltpu.VMEM((1, 2 * tq), F32),
            pltpu.VMEM((DV + ONES_ROWS, 2 * tq), F32)]


def _sample_phases(qrow_ref, qcol_ref, kn_ref, vn_ref, ga_ref, lq1, lk1, lq2, lk2, subg_ref,
                   expand_ref, k_pages, v_pages, o_ref, sc_ref, p_ref, pn_ref, acc_ref):
    pages = len(k_pages)
    n_maps = 2 * N_HEADS

    def scores(c):
        qcol = qcol_ref[0]
        for j, kp in enumerate(k_pages):
            prod = (kp[...] * qcol).reshape(N_HEADS, 2 * DH, CHUNK)
            lanes = slice(j * CHUNK, (j + 1) * CHUNK)
            sc_ref[c, :N_HEADS, lanes] = jnp.sum(prod[:, :DH, :], axis=1)
            sc_ref[c, N_HEADS:, lanes] = jnp.sum(prod[:, DH:, :], axis=1)

    def softmax():
        s = sc_ref[...]
        row = lax.broadcasted_iota(jnp.int32, (n_maps, D_ATTN), 0)
        seg = lax.broadcasted_iota(jnp.int32, (n_maps, D_ATTN), 1) // DH
        want = jnp.where(row < N_HEADS, 2 * row, 2 * (row - N_HEADS) + 1)
        qk_new = jnp.broadcast_to(qrow_ref[0] * kn_ref[0], (n_maps, D_ATTN))
        s_new = jnp.sum(jnp.where(seg == want, qk_new, 0.0), axis=-1, keepdims=True)
        m = jnp.max(jnp.max(s, axis=-1, keepdims=True), axis=0)
        m = jnp.maximum(m, s_new)
        p = jnp.exp2(s - m[None])
        pn = jnp.exp2(s_new - m)
        l = jnp.sum(jnp.sum(p, axis=-1, keepdims=True), axis=0) + pn
        inv_l = 1.0 / l
        p_ref[...] = (p * inv_l[None]).astype(BF16)
        pn_ref[...] = pn * inv_l
        acc_ref[...] = jnp.zeros(acc_ref.shape, F32)

    def values(c):
        pstack = jnp.concatenate(
            [p_ref[c, :, j * CHUNK:(j + 1) * CHUNK] for j in range(pages)], axis=0)
        pexp = jnp.dot(pstack, expand_ref[...], preferred_element_type=F32)
        row = lax.broadcasted_iota(jnp.int32, pexp.shape, 0)
        col = lax.broadcasted_iota(jnp.int32, pexp.shape, 1)
        pexp = jnp.where(row % N_HEADS == col % N_HEADS, pexp, 0.0).astype(BF16)
        acc = acc_ref[...]
        for j, vp in enumerate(v_pages):
            acc += jnp.dot(pexp[j * n_maps:(j + 1) * n_maps, :], vp[...].astype(BF16),
                           preferred_element_type=F32)
        acc_ref[...] = acc

    def finish():
        lam = _lam(lq1, lk1, lq2, lk2)
        v_new = vn_ref[0]
        full = acc_ref[...] + pn_ref[...] * jnp.concatenate([v_new, v_new], axis=0)
        o = full[:N_HEADS, :] - lam * full[N_HEADS:, :]
        ms = jnp.mean(o * o, axis=-1, keepdims=True)
        on = o * lax.rsqrt(ms + EPS) * subg_ref[...] * (1.0 - LAM_INIT)
        o_ref[0] = (on * _silu(ga_ref[0])).astype(o_ref.dtype)

    return scores, softmax, values, finish


def _sample_attn_kernel(pt_ref, qrow_ref, qcol_ref, kn_ref, vn_ref, ga_ref, lq1, lk1, lq2, lk2,
                        subg_ref, expand_ref, *rest, pages, n_chunks):
    del pt_ref
    scores, softmax, values, finish = _sample_phases(
        qrow_ref, qcol_ref, kn_ref, vn_ref, ga_ref, lq1, lk1, lq2, lk2, subg_ref, expand_ref,
        rest[:pages], rest[pages:2 * pages], *rest[2 * pages:])
    c = pl.program_id(1)
    pl.when(c < n_chunks)(lambda: scores(c))
    pl.when(c == n_chunks - 1)(softmax)
    pl.when(c >= n_chunks)(lambda: values(c - n_chunks))
    pl.when(c == 2 * n_chunks - 1)(finish)


def _fused_attn_kernel(pt_ref, q_ref, k_ref, v_ref, gap_ref, lq1, lk1, lq2, lk2, subg_ref,
                       qrow_ref, qcol_ref, kn_ref, vn_ref, gas_ref, expand_ref, *rest,
                       tq, tk, pages, n_chunks, n_seq):
    del pt_ref
    k_pages, v_pages = rest[:pages], rest[pages:2 * pages]
    op_ref, os_ref = rest[2 * pages:2 * pages + 2]
    prompt_scratch = rest[2 * pages + 2:2 * pages + 8]
    sample_scratch = rest[2 * pages + 8:]
    g = pl.program_id(0) * pl.num_programs(1) + pl.program_id(1)
    seq = g // n_chunks
    c = g % n_chunks
    scores, softmax, values, finish = _sample_phases(
        qrow_ref, qcol_ref, kn_ref, vn_ref, gas_ref, lq1, lk1, lq2, lk2, subg_ref, expand_ref,
        k_pages, v_pages, os_ref, *sample_scratch)
    last = c == n_chunks - 1
    trailing = (seq >= 1) & (seq <= n_seq)
    pl.when(trailing)(lambda: values(c))
    pl.when(trailing & last)(finish)
    pl.when(seq < n_seq)(lambda: scores(c))
    pl.when((seq < n_seq) & last)(softmax)

    _prompt_attn_step(pl.program_id(1), q_ref, k_ref, v_ref, gap_ref, lq1, lk1, lq2, lk2,
                      subg_ref, op_ref, *prompt_scratch, tq=tq, tk=tk)


def _sample_operands(page_table, q_s, k_s, v_s, ga_s, cache_k, cache_v):
    db = q_s.shape[0]
    n_pool = cache_k.shape[0]
    ck = jnp.transpose(cache_k, (0, 2, 3, 4, 1)).reshape(n_pool, D_ATTN, CHUNK)
    cv = cache_v.reshape(n_pool, CHUNK * N_HEADS, DV)
    expand = (jnp.arange(CHUNK * N_HEADS)[None, :] // N_HEADS
              == jnp.arange(CHUNK)[:, None]).astype(BF16)
    per_seq = (q_s.reshape(db, 1, D_ATTN),
               jnp.broadcast_to(q_s[:, :, None], (db, D_ATTN, LANES)),
               k_s.reshape(db, 1, D_ATTN),
               v_s.reshape(db, N_HEADS, DV),
               ga_s.reshape(db, N_HEADS, DV))
    return page_table.reshape(-1), per_seq, expand, ck, cv


_PER_SEQ_BLOCKS = ((1, D_ATTN), (D_ATTN, LANES), (1, D_ATTN), (N_HEADS, DV), (N_HEADS, DV))


def _sample_scratch(n_chunks, span):
    return [pltpu.VMEM((n_chunks, 2 * N_HEADS, span), F32),
            pltpu.VMEM((n_chunks, 2 * N_HEADS, span), BF16),
            pltpu.VMEM((2 * N_HEADS, 1), F32),
            pltpu.VMEM((2 * N_HEADS, DV), F32)]


def _sample_attn(operands, lq1, lk1, lq2, lk2, subg, *, first, count, n_pages, pages):
    pt, per_seq, expand, ck, cv = operands
    n_chunks = n_pages // pages

    def page(b, chunk, j, pt_):
        return pt_[(first + b) * n_pages + chunk * pages + j]

    per_b = lambda shp: pl.BlockSpec((1,) + shp, lambda b, c, pt_: (first + b, 0, 0))
    kspec = lambda j: pl.BlockSpec(
        (None, D_ATTN, CHUNK),
        lambda b, c, pt_: (page(b, jnp.minimum(c, n_chunks - 1), j, pt_), 0, 0))
    vspec = lambda j: pl.BlockSpec(
        (None, CHUNK * N_HEADS, DV),
        lambda b, c, pt_: (page(b, jnp.maximum(c - n_chunks, 0), j, pt_), 0, 0))
    small = lambda n: pl.BlockSpec((1, n), lambda b, c, pt_: (0, 0))
    grid_spec = pltpu.PrefetchScalarGridSpec(
        num_scalar_prefetch=1,
        grid=(count, 2 * n_chunks),
        in_specs=[per_b(shp) for shp in _PER_SEQ_BLOCKS]
                 + [small(DH), small(DH), small(DH), small(DH), small(DV),
                    pl.BlockSpec((CHUNK, CHUNK * N_HEADS), lambda b, c, pt_: (0, 0))]
                 + [kspec(j) for j in range(pages)] + [vspec(j) for j in range(pages)],
        out_specs=pl.BlockSpec((1, N_HEADS, DV), lambda b, c, pt_: (b, 0, 0)),
        scratch_shapes=_sample_scratch(n_chunks, pages * CHUNK),
    )
    return pl.pallas_call(
        functools.partial(_sample_attn_kernel, pages=pages, n_chunks=n_chunks),
        grid_spec=grid_spec,
        out_shape=jax.ShapeDtypeStruct((count, N_HEADS, DV), BF16),
        compiler_params=pltpu.CompilerParams(
            dimension_semantics=("parallel", "arbitrary"),
            vmem_limit_bytes=48 * MIB),
        name="sample_attn",
    )(pt, *per_seq, lq1, lk1, lq2, lk2, subg, expand, *([ck] * pages), *([cv] * pages))


def _fused_attn(z, kh, vh, operands, lq1, lk1, lq2, lk2, subg, *, tq, n_pages, pages):
    pt, per_seq, expand, ck, cv = operands
    s = z.shape[0]
    tk = tq // 2
    n_q = s // tq
    n_chunks = n_pages // pages
    db = per_seq[0].shape[0]
    n_seq = min(db, N_HEADS * n_q // n_chunks - 1)

    def step(h, i):
        return h * n_q + i

    def key_page(h, i, j, pt_):
        g = jnp.minimum(step(h, i), n_seq * n_chunks - 1)
        return pt_[(g // n_chunks) * n_pages + (g % n_chunks) * pages + j]

    def value_page(h, i, j, pt_):
        g = jnp.clip(step(h, i) - n_chunks, 0, n_seq * n_chunks - 1)
        return pt_[(g // n_chunks) * n_pages + (g % n_chunks) * pages + j]

    key_seq = lambda h, i: jnp.minimum(step(h, i) // n_chunks, n_seq - 1)
    value_seq = lambda h, i: jnp.clip(step(h, i) // n_chunks - 1, 0, n_seq - 1)
    seq_of = (key_seq, key_seq, key_seq, value_seq, value_seq)
    per_b = [pl.BlockSpec((1,) + shp, lambda h, i, pt_, f=f: (f(h, i), 0, 0))
             for shp, f in zip(_PER_SEQ_BLOCKS, seq_of)]
    kspec = lambda j: pl.BlockSpec((None, D_ATTN, CHUNK),
                                   lambda h, i, pt_: (key_page(h, i, j, pt_), 0, 0))
    vspec = lambda j: pl.BlockSpec((None, CHUNK * N_HEADS, DV),
                                   lambda h, i, pt_: (value_page(h, i, j, pt_), 0, 0))
    small = lambda n: pl.BlockSpec((1, n), lambda h, i, pt_: (0, 0))
    grid_spec = pltpu.PrefetchScalarGridSpec(
        num_scalar_prefetch=1,
        grid=(N_HEADS, n_q),
        in_specs=[pl.BlockSpec((tq, LANES), lambda h, i, pt_: (i, h)),
                  pl.BlockSpec((None, s, LANES), lambda h, i, pt_: (h, 0, 0)),
                  pl.BlockSpec((None, s, LANES), lambda h, i, pt_: (h, 0, 0)),
                  pl.BlockSpec((tq, LANES), lambda h, i, pt_: (i, 3 * N_HEADS + h)),
                  small(DH), small(DH), small(DH), small(DH), small(DV)]
                 + per_b
                 + [pl.BlockSpec((CHUNK, CHUNK * N_HEADS), lambda h, i, pt_: (0, 0))]
                 + [kspec(j) for j in range(pages)] + [vspec(j) for j in range(pages)],
        out_specs=[pl.BlockSpec((tq, LANES), lambda h, i, pt_: (i, h)),
                   pl.BlockSpec((1, N_HEADS, DV), lambda h, i, pt_: (value_seq(h, i), 0, 0))],
        scratch_shapes=_prompt_attn_scratch(s, tq, tk) + _sample_scratch(n_chunks, pages * CHUNK),
    )
    a_p, a_s = pl.pallas_call(
        functools.partial(_fused_attn_kernel, tq=tq, tk=tk, pages=pages, n_chunks=n_chunks,
                          n_seq=n_seq),
        grid_spec=grid_spec,
        out_shape=[jax.ShapeDtypeStruct((s, D_ATTN), BF16),
                   jax.ShapeDtypeStruct((n_seq, N_HEADS, DV), BF16)],
        compiler_params=pltpu.CompilerParams(
            dimension_semantics=("arbitrary", "arbitrary"),
            vmem_limit_bytes=58 * MIB),
        name="fused_attn",
    )(pt, z, kh, vh, z, lq1, lk1, lq2, lk2, subg, *per_seq, expand,
      *([ck] * pages), *([cv] * pages))
    return a_p, a_s, n_seq


def _layernorm(x, g, b):
    mu = jnp.mean(x, axis=-1, keepdims=True)
    xc = x - mu
    var = jnp.mean(xc * xc, axis=-1, keepdims=True)
    return xc * lax.rsqrt(var + EPS) * g + b


def _sgate_prompt_kernel(u_ref, vsg_ref, gb_ref, lng_ref, lnb_ref, ws_ref, bias_ref, o_ref):
    rows = u_ref.shape[0]
    n_c = rows // CHUNK
    vn = _layernorm(vsg_ref[...].astype(F32), lng_ref[...], lnb_ref[...]).astype(BF16)
    t = lax.broadcasted_iota(jnp.int32, (CHUNK, CHUNK), 0)
    s = lax.broadcasted_iota(jnp.int32, (CHUNK, CHUNK), 1)
    causal = s <= t
    for g in range(N_GROUPS):
        cols = slice(g * SG_GROUP, (g + 1) * SG_GROUP)
        wm = jnp.where(causal, ws_ref[g], 0.0).astype(BF16)
        rhs = jnp.concatenate([vn[c * CHUNK:(c + 1) * CHUNK, cols] for c in range(n_c)], axis=1)
        sp = jnp.dot(wm, rhs, preferred_element_type=F32)
        for c in range(n_c):
            rws = slice(c * CHUNK, (c + 1) * CHUNK)
            sp_c = sp[:, c * SG_GROUP:(c + 1) * SG_GROUP] + bias_ref[:, cols]
            gb = gb_ref[rws, cols].astype(F32)
            o_ref[rws, cols] = (u_ref[rws, cols].astype(F32) * sp_c * _silu(gb)).astype(o_ref.dtype)


def _sgate_prompt(z, ln_g, ln_b, w_s, bias, *, rows):
    s = z.shape[0]
    const2 = lambda shp: pl.BlockSpec(shp, lambda i: (0, 0))
    return pl.pallas_call(
        _sgate_prompt_kernel,
        grid=(s // rows,),
        in_specs=[
            pl.BlockSpec((rows, D_SG), lambda i: (i, 4)),
            pl.BlockSpec((rows, D_SG), lambda i: (i, 5)),
            pl.BlockSpec((rows, D_SG), lambda i: (i, 6)),
            const2((1, D_SG)), const2((1, D_SG)),
            pl.BlockSpec((N_GROUPS, CHUNK, CHUNK), lambda i: (0, 0, 0)),
            const2((CHUNK, D_SG)),
        ],
        out_specs=pl.BlockSpec((rows, D_SG), lambda i: (i, 0)),
        out_shape=jax.ShapeDtypeStruct((s, D_SG), BF16),
        compiler_params=pltpu.CompilerParams(dimension_semantics=("parallel",),
                                             vmem_limit_bytes=48 * MIB),
        name="sgate_prompt",
    )(z, z, z, ln_g, ln_b, w_s, bias)


def _sgate_sample_kernel(u_ref, vsg_ref, gb_ref, lng_ref, lnb_ref, w0_ref, b0_ref, o_ref, vn_ref):
    vn = _layernorm(vsg_ref[...], lng_ref[...], lnb_ref[...])
    vn_ref[...] = vn
    sp = w0_ref[...] * vn + b0_ref[...]
    o_ref[...] = (u_ref[...] * sp * _silu(gb_ref[...])).astype(o_ref.dtype)


def _sgate_sample(zs, ln_g, ln_b, w0, b0):
    db = zs.shape[0]
    const2 = lambda shp: pl.BlockSpec(shp, lambda i: (0, 0))
    return pl.pallas_call(
        _sgate_sample_kernel,
        grid=(1,),
        in_specs=[
            pl.BlockSpec((db, D_SG), lambda i: (0, 4)),
            pl.BlockSpec((db, D_SG), lambda i: (0, 5)),
            pl.BlockSpec((db, D_SG), lambda i: (0, 6)),
            const2((1, D_SG)), const2((1, D_SG)), const2((1, D_SG)), const2((1, D_SG)),
        ],
        out_specs=[pl.BlockSpec((db, D_SG), lambda i: (0, 0)),
                   pl.BlockSpec((db, D_SG), lambda i: (0, 0))],
        out_shape=[jax.ShapeDtypeStruct((db, D_SG), BF16),
                   jax.ShapeDtypeStruct((db, D_SG), F32)],
        name="sgate_sample",
    )(zs, zs, zs, ln_g, ln_b, w0, b0)


def _merge_kernel(x_ref, a_ref, b_ref, ga0_ref, ga1_ref, gb0_ref, gb1_ref,
                  wpa_ref, wpb_ref, wo_ref, np_ref, o_ref):
    ya = jnp.dot(a_ref[...], wpa_ref[...], preferred_element_type=F32)
    yb = jnp.dot(b_ref[...], wpb_ref[...], preferred_element_type=F32)
    half = D_MODEL // 2
    sig = lambda r: jax.nn.sigmoid(r[...].astype(F32))
    m_lo = sig(ga0_ref) * ya[:, :half] + sig(gb0_ref) * yb[:, :half]
    m_hi = sig(ga1_ref) * ya[:, half:] + sig(gb1_ref) * yb[:, half:]
    m = jnp.concatenate([m_lo, m_hi], axis=1).astype(BF16)
    mo = jnp.dot(m, wo_ref[...], preferred_element_type=F32)
    ms = jnp.mean(mo * mo, axis=-1, keepdims=True)
    o_ref[...] = x_ref[...] + mo * lax.rsqrt(ms + EPS) * np_ref[...]


def _merge(x, a_in, b_in, z, wpa_bf, wpb_bf, wo_bf, norm_post, *, tm):
    m = x.shape[0]
    resident = lambda shp: pl.BlockSpec(shp, lambda i: (0, 0), pipeline_mode=pl.Buffered(1))
    zcol = lambda jcol: pl.BlockSpec((tm, COL_BLOCK), lambda i: (i, jcol))
    return pl.pallas_call(
        _merge_kernel,
        grid=(m // tm,),
        in_specs=[
            pl.BlockSpec((tm, D_MODEL), lambda i: (i, 0)),
            pl.BlockSpec((tm, D_ATTN), lambda i: (i, 0)),
            pl.BlockSpec((tm, D_SG), lambda i: (i, 0)),
            zcol(7), zcol(8), zcol(9), zcol(10),
            resident((D_ATTN, D_MODEL)), resident((D_SG, D_MODEL)), resident((D_MODEL, D_MODEL)),
            resident((1, D_MODEL)),
        ],
        out_specs=pl.BlockSpec((tm, D_MODEL), lambda i: (i, 0)),
        out_shape=jax.ShapeDtypeStruct((m, D_MODEL), F32),
        compiler_params=pltpu.CompilerParams(dimension_semantics=("parallel",),
                                             vmem_limit_bytes=56 * MIB),
        name="merge",
    )(x, a_in, b_in, z, z, z, z, wpa_bf, wpb_bf, wo_bf, norm_post)


def _rope_tables(pos):
    half = DH // 2
    inv = ROPE_THETA ** (-(jnp.arange(half, dtype=F32) * 2.0 / DH))
    ang = pos.astype(F32)[:, None] * inv[None, :]
    cos = jnp.cos(ang)
    sin = jnp.sin(ang)
    reps = LANES // DH
    cos_t = jnp.tile(jnp.concatenate([cos, cos], axis=1), (1, reps))
    sin_t = jnp.tile(jnp.concatenate([-sin, sin], axis=1), (1, reps))
    return cos_t, sin_t


def kernel(x_prompt, x_sample, cache_k, cache_v, page_table, norm_pre, w_in, lam_q1, lam_k1,
           lam_q2, lam_k2, sub_g, w_pa, ln_g, ln_b, w_s, b_s, w_pb, w_o, norm_post):
    b, s, _ = x_prompt.shape
    db, t, _ = x_sample.shape
    assert b == 1 and t == 1 and norm_pre.shape[0] == 1
    n_past = page_table.shape[1] * cache_k.shape[2]

    wpa_bf = w_pa[0].astype(BF16)
    wpb_bf = w_pb[0].astype(BF16)
    wo_bf = w_o[0].astype(BF16)
    lams = (lam_q1, lam_k1, lam_q2, lam_k2)

    xs = x_sample.reshape(db, D_MODEL)
    cos_s, sin_s = _rope_tables(jnp.full((db,), n_past, dtype=jnp.int32))
    z_s, k_s, v_s, _, _, w_in_bf = _inproj(xs, norm_pre, w_in[0], cos_s, sin_s, tm=db,
                                           z_dtype=F32)
    sample_ops = _sample_operands(page_table, z_s[:, :D_ATTN], k_s, v_s,
                                  z_s[:, 3 * D_ATTN:4 * D_ATTN], cache_k[0], cache_v[0])
    n_pages = page_table.shape[1]

    xp = x_prompt.reshape(s, D_MODEL)
    cos_p, sin_p = _rope_tables(jnp.arange(s, dtype=jnp.int32))
    z_p, k_p, v_p, kh_p, vh_p = _inproj(xp, norm_pre, w_in_bf, cos_p, sin_p, tm=1024,
                                        z_dtype=BF16)
    a_p, a_s, n_seq = _fused_attn(z_p, kh_p, vh_p, sample_ops, *lams, sub_g, tq=512,
                                  n_pages=n_pages, pages=SAMPLE_PAGES)
    bias_p = jnp.repeat(jnp.transpose(b_s[0]), SG_GROUP, axis=1)
    b_p = _sgate_prompt(z_p, ln_g, ln_b, w_s[0], bias_p, rows=512)
    y_p = _merge(xp, a_p, b_p, z_p, wpa_bf, wpb_bf, wo_bf, norm_post, tm=256)

    if n_seq < db:
        a_rest = _sample_attn(sample_ops, *lams, sub_g, first=n_seq, count=db - n_seq,
                              n_pages=n_pages, pages=SAMPLE_PAGES)
        a_s = jnp.concatenate([a_s, a_rest], axis=0)
    a_s = a_s.reshape(db, D_ATTN)
    w0 = jnp.repeat(w_s[0, :, 0, 0], SG_GROUP)[None, :]
    b0 = jnp.repeat(b_s[0, :, 0], SG_GROUP)[None, :]
    b_sm, vn_s = _sgate_sample(z_s, ln_g, ln_b, w0, b0)
    y_s = _merge(xs, a_s, b_sm, z_s, wpa_bf, wpb_bf, wo_bf, norm_post, tm=db)

    return (y_p.reshape(b, s, D_MODEL),
            y_s.reshape(db, t, D_MODEL),
            k_p.reshape(1, b, s, N_HEADS, 2, DH),
            v_p.reshape(1, b, s, N_HEADS, DV),
            k_s.reshape(1, db, t, N_HEADS, 2, DH),
            v_s.reshape(1, db, t, N_HEADS, DV),
            vn_s.reshape(1, db, t, D_SG))
```

```python
import functools
import math

import jax
import jax.numpy as jnp
from jax import lax
from jax.experimental import pallas as pl
from jax.experimental.pallas import tpu as pltpu

F32 = jnp.float32
BF16 = jnp.bfloat16

D_MODEL = 2048
D_ATTN = D_MODEL // 2
N_HEADS = 8
DH = D_ATTN // (2 * N_HEADS)
DV = 2 * DH
D_SG = D_MODEL // 2
N_GROUPS = 8
SG_GROUP = D_SG // N_GROUPS
CHUNK = 128
ROPE_THETA = 10000.0
EPS = 1e-6
NEG = -1e30
D_IN = 4 * D_ATTN + 3 * D_SG + 2 * D_MODEL
COL_BLOCK = 1024
N_COL_BLOCKS = D_IN // COL_BLOCK
Z_Q, Z_GATE_A, Z_U, Z_VSG, Z_GATE_B, Z_GA, Z_GB = 0, 1, 2, 3, 4, 5, 7
Z_WIDTH = D_IN - 2 * D_ATTN
LANES = 128
Q_SCALE = DH ** -0.5 * math.log2(math.e)
LAM_INIT = 0.8 - 0.6 * math.exp(-0.3 * 0)
MIB = 1024 * 1024
ONES_ROWS = 16
SAMPLE_PAGES = 16


def _silu(x):
    return x * jax.nn.sigmoid(x)


def _lam(lq1, lk1, lq2, lk2):
    a = jnp.sum(lq1[...] * lk1[...], axis=-1, keepdims=True)
    b = jnp.sum(lq2[...] * lk2[...], axis=-1, keepdims=True)
    return jnp.exp(a) - jnp.exp(b) + LAM_INIT


def _inproj_kernel(x_ref, g_ref, w_ref, cos_ref, sin_ref, z_ref, k_ref, v_ref, kh_ref, vh_ref,
                   *rest):
    xn_ref = rest[-1]

    def by_head(dst, a):
        for h in range(N_HEADS):
            dst[h] = a[:, h * LANES:(h + 1) * LANES].astype(dst.dtype)

    j = pl.program_id(1)
    tm = x_ref.shape[0]

    @pl.when(j == 0)
    def _():
        x = x_ref[...]
        ms = jnp.mean(x * x, axis=-1, keepdims=True)
        xn_ref[...] = (x * lax.rsqrt(ms + EPS) * g_ref[...]).astype(BF16)

    w = w_ref[...].astype(BF16)
    if len(rest) == 2:
        rest[0][...] = w
    acc = jnp.dot(xn_ref[...], w, preferred_element_type=F32)

    def rope(a):
        cos = cos_ref[...]
        sin = sin_ref[...]
        lane = lax.broadcasted_iota(jnp.int32, (tm, LANES), 1)
        first_half = (lane % DH) < (DH // 2)
        outs = []
        for c in range(COL_BLOCK // LANES):
            blk = a[:, c * LANES:(c + 1) * LANES]
            partner = jnp.where(first_half,
                                pltpu.roll(blk, LANES - DH // 2, 1),
                                pltpu.roll(blk, DH // 2, 1))
            outs.append(blk * cos + partner * sin)
        return jnp.concatenate(outs, axis=1)

    @pl.when(j == 0)
    def _():
        z_ref[...] = (rope(acc) * Q_SCALE).astype(z_ref.dtype)

    @pl.when(j == 1)
    def _():
        r = rope(acc)
        k_ref[...] = r
        by_head(kh_ref, r)

    @pl.when(j == 2)
    def _():
        v_ref[...] = acc
        by_head(vh_ref, acc)

    @pl.when(j > 2)
    def _():
        z_ref[...] = acc.astype(z_ref.dtype)


def _inproj(x, g, w_bf, cos_t, sin_t, *, tm, z_dtype):
    m = x.shape[0]
    emit_w = w_bf.dtype != BF16
    assert not emit_w or m == tm
    w_out_spec = [pl.BlockSpec((D_MODEL, COL_BLOCK), lambda i, j: (0, j))] if emit_w else []
    w_out_shape = [jax.ShapeDtypeStruct((D_MODEL, D_IN), BF16)] if emit_w else []
    return pl.pallas_call(
        _inproj_kernel,
        grid=(m // tm, N_COL_BLOCKS),
        in_specs=[
            pl.BlockSpec((tm, D_MODEL), lambda i, j: (i, 0), pipeline_mode=pl.Buffered(1)),
            pl.BlockSpec((1, D_MODEL), lambda i, j: (0, 0)),
            pl.BlockSpec((D_MODEL, COL_BLOCK), lambda i, j: (0, j)),
            pl.BlockSpec((tm, LANES), lambda i, j: (i, 0)),
            pl.BlockSpec((tm, LANES), lambda i, j: (i, 0)),
        ],
        out_specs=[
            pl.BlockSpec((tm, COL_BLOCK), lambda i, j: (i, jnp.maximum(j - 2, 0))),
            pl.BlockSpec((tm, COL_BLOCK), lambda i, j: (i, 0), pipeline_mode=pl.Buffered(1)),
            pl.BlockSpec((tm, COL_BLOCK), lambda i, j: (i, 0), pipeline_mode=pl.Buffered(1)),
            pl.BlockSpec((N_HEADS, tm, LANES), lambda i, j: (0, i, 0), pipeline_mode=pl.Buffered(1)),
            pl.BlockSpec((N_HEADS, tm, LANES), lambda i, j: (0, i, 0), pipeline_mode=pl.Buffered(1)),
        ] + w_out_spec,
        out_shape=[
            jax.ShapeDtypeStruct((m, Z_WIDTH), z_dtype),
            jax.ShapeDtypeStruct((m, D_ATTN), F32),
            jax.ShapeDtypeStruct((m, D_ATTN), F32),
            jax.ShapeDtypeStruct((N_HEADS, m, LANES), BF16),
            jax.ShapeDtypeStruct((N_HEADS, m, LANES), BF16),
        ] + w_out_shape,
        scratch_shapes=[pltpu.VMEM((tm, D_MODEL), BF16)],
        compiler_params=pltpu.CompilerParams(
            dimension_semantics=("parallel", "arbitrary"),
            vmem_limit_bytes=56 * MIB),
        name="inproj",
    )(x, g, w_bf, cos_t, sin_t)


def _prompt_attn_step(qi, q_ref, k_ref, v_ref, ga_ref, lq1, lk1, lq2, lk2, subg_ref, o_ref,
                      vt_ref, qq_ref, sa_ref, sb_ref, m_ref, acc_ref, *, tq, tk):
    n_kv = k_ref.shape[0] // tk

    @pl.when(qi == 0)
    def _():
        def transpose_block(kb, carry):
            start = pl.multiple_of(kb * tk, tk)
            vt_ref[kb, :DV, :] = v_ref[pl.ds(start, tk), :].astype(F32).T.astype(BF16)
            vt_ref[kb, DV:, :] = jnp.ones((ONES_ROWS, tk), BF16)
            return carry
        lax.fori_loop(0, n_kv, transpose_block, 0)

    q_t = q_ref[...].astype(F32).T
    feat = lax.broadcasted_iota(jnp.int32, (LANES, tq), 0)
    qq_ref[:, :tq] = jnp.where(feat < DH, q_t, 0.0).astype(BF16)
    qq_ref[:, tq:] = jnp.where(feat >= DH, q_t, 0.0).astype(BF16)

    m_ref[...] = jnp.full(m_ref.shape, NEG, F32)
    acc_ref[...] = jnp.zeros(acc_ref.shape, F32)

    def scores(kb, dst):
        start = pl.multiple_of(kb * tk, tk)
        dst[...] = jnp.dot(k_ref[pl.ds(start, tk), :], qq_ref[...],
                           preferred_element_type=F32)

    def consume(src, kb, masked):
        s_t = src[...]
        if masked:
            key = kb * tk + lax.broadcasted_iota(jnp.int32, (tk, 2 * tq), 0)
            qry = lax.broadcasted_iota(jnp.int32, (tk, 2 * tq), 1)
            qry = qi * tq + jnp.where(qry >= tq, qry - tq, qry)
            s_t = jnp.where(key <= qry, s_t, NEG)
        m_old = m_ref[...]
        m_new = jnp.maximum(m_old, jnp.max(s_t, axis=0, keepdims=True))
        alpha = jnp.exp2(m_old - m_new)
        p_t = jnp.exp2(s_t - m_new)
        acc_ref[...] = alpha * acc_ref[...] + jnp.dot(vt_ref[kb], p_t.astype(BF16),
                                                      preferred_element_type=F32)
        m_ref[...] = m_new

    assert tq == 2 * tk
    scores(0, sa_ref)

    def pair(first):
        scores(first + 1, sb_ref)
        consume(sa_ref, first, False)
        scores(first + 2, sa_ref)
        consume(sb_ref, first + 1, False)

    def two_pairs(p, carry):
        pair(4 * p)
        pair(4 * p + 2)
        return carry

    lax.fori_loop(0, qi // 2, two_pairs, 0)

    @pl.when(qi % 2 == 1)
    def _():
        pair(2 * qi - 2)

    scores(2 * qi + 1, sb_ref)
    consume(sa_ref, 2 * qi, True)
    consume(sb_ref, 2 * qi + 1, True)

    lam = _lam(lq1, lk1, lq2, lk2)
    o_all = acc_ref[:DV, :] / acc_ref[DV:DV + 1, :]
    o = (o_all[:, :tq] - lam * o_all[:, tq:]).T
    ms = jnp.mean(o * o, axis=-1, keepdims=True)
    on = o * lax.rsqrt(ms + EPS) * subg_ref[...] * (1.0 - LAM_INIT)
    ga = ga_ref[...].astype(F32)
    o_ref[...] = (on * _silu(ga)).astype(o_ref.dtype)


def _prompt_attn_scratch(s, tq, tk):
    return [pltpu.VMEM((s // tk, DV + ONES_ROWS, tk), BF16),
            pltpu.VMEM((LANES, 2 * tq), BF16),
            pltpu.VMEM((tk, 2 * tq), F32), pltpu.VMEM((tk, 2 * tq), F32),
            pltpu.VMEM((1, 2 * tq), F32),
            pltpu.VMEM((DV + ONES_ROWS, 2 * tq), F32)]


def _sample_phases(qrow_ref, qcol_ref, kn_ref, vn_ref, ga_ref, lq1, lk1, lq2, lk2, subg_ref,
                   expand_ref, k_pages, v_pages, o_ref, sc_ref, p_ref, pn_ref, acc_ref):
    pages = len(k_pages)
    n_maps = 2 * N_HEADS

    def scores(c):
        for h in range(N_HEADS):
            rows = slice(h * 2 * DH, (h + 1) * 2 * DH)
            q_h = qcol_ref[0, rows, :]
            for j, kp in enumerate(k_pages):
                prod = kp[rows, :] * q_h
                lanes = slice(j * CHUNK, (j + 1) * CHUNK)
                sc_ref[c, h:h + 1, lanes] = jnp.sum(prod[:DH, :], axis=0, keepdims=True)
                sc_ref[c, N_HEADS + h:N_HEADS + h + 1, lanes] = jnp.sum(
                    prod[DH:, :], axis=0, keepdims=True)

    def softmax():
        s = sc_ref[...]
        row = lax.broadcasted_iota(jnp.int32, (n_maps, D_ATTN), 0)
        seg = lax.broadcasted_iota(jnp.int32, (n_maps, D_ATTN), 1) // DH
        want = jnp.where(row < N_HEADS, 2 * row, 2 * (row - N_HEADS) + 1)
        qk_new = jnp.broadcast_to(qrow_ref[0] * kn_ref[0], (n_maps, D_ATTN))
        s_new = jnp.sum(jnp.where(seg == want, qk_new, 0.0), axis=-1, keepdims=True)
        m = jnp.max(jnp.max(s, axis=-1, keepdims=True), axis=0)
        m = jnp.maximum(m, s_new)
        p = jnp.exp2(s - m[None])
        pn = jnp.exp2(s_new - m)
        l = jnp.sum(jnp.sum(p, axis=-1, keepdims=True), axis=0) + pn
        inv_l = 1.0 / l
        p_ref[...] = (p * inv_l[None]).astype(BF16)
        pn_ref[...] = pn * inv_l
        acc_ref[...] = jnp.zeros(acc_ref.shape, F32)

    def values(c):
        pstack = jnp.concatenate(
            [p_ref[c, :, j * CHUNK:(j + 1) * CHUNK] for j in range(pages)], axis=0)
        pexp = jnp.dot(pstack, expand_ref[...], preferred_element_type=F32)
        row = lax.broadcasted_iota(jnp.int32, pexp.shape, 0)
        col = lax.broadcasted_iota(jnp.int32, pexp.shape, 1)
        pexp = jnp.where(row % N_HEADS == col % N_HEADS, pexp, 0.0).astype(BF16)
        acc = acc_ref[...]
        for j, vp in enumerate(v_pages):
            acc += jnp.dot(pexp[j * n_maps:(j + 1) * n_maps, :], vp[...].astype(BF16),
                           preferred_element_type=F32)
        acc_ref[...] = acc

    def finish():
        lam = _lam(lq1, lk1, lq2, lk2)
        v_new = vn_ref[0]
        full = acc_ref[...] + pn_ref[...] * jnp.concatenate([v_new, v_new], axis=0)
        o = full[:N_HEADS, :] - lam * full[N_HEADS:, :]
        ms = jnp.mean(o * o, axis=-1, keepdims=True)
        on = o * lax.rsqrt(ms + EPS) * subg_ref[...] * (1.0 - LAM_INIT)
        o_ref[0] = (on * _silu(ga_ref[0])).astype(o_ref.dtype)

    return scores, softmax, values, finish


def _sample_attn_kernel(pt_ref, qrow_ref, qcol_ref, kn_ref, vn_ref, ga_ref, lq1, lk1, lq2, lk2,
                        subg_ref, expand_ref, *rest, pages, n_chunks):
    del pt_ref
    scores, softmax, values, finish = _sample_phases(
        qrow_ref, qcol_ref, kn_ref, vn_ref, ga_ref, lq1, lk1, lq2, lk2, subg_ref, expand_ref,
        rest[:pages], rest[pages:2 * pages], *rest[2 * pages:])
    c = pl.program_id(1)
    pl.when(c < n_chunks)(lambda: scores(c))
    pl.when(c == n_chunks - 1)(softmax)
    pl.when(c >= n_chunks)(lambda: values(c - n_chunks))
    pl.when(c == 2 * n_chunks - 1)(finish)


def _fused_attn_kernel(pt_ref, q_ref, k_ref, v_ref, gap_ref, lq1, lk1, lq2, lk2, subg_ref,
                       qrow_ref, qcol_ref, kn_ref, vn_ref, gas_ref, expand_ref, *rest,
                       tq, tk, pages, n_chunks, n_seq):
    del pt_ref
    k_pages, v_pages = rest[:pages], rest[pages:2 * pages]
    op_ref, os_ref = rest[2 * pages:2 * pages + 2]
    prompt_scratch = rest[2 * pages + 2:2 * pages + 8]
    sample_scratch = rest[2 * pages + 8:]
    g = pl.program_id(0) * pl.num_programs(1) + pl.program_id(1)
    seq = g // n_chunks
    c = g % n_chunks
    scores, softmax, values, finish = _sample_phases(
        qrow_ref, qcol_ref, kn_ref, vn_ref, gas_ref, lq1, lk1, lq2, lk2, subg_ref, expand_ref,
        k_pages, v_pages, os_ref, *sample_scratch)
    last = c == n_chunks - 1
    trailing = (seq >= 1) & (seq <= n_seq)
    pl.when(trailing)(lambda: values(c))
    pl.when(trailing & last)(finish)
    pl.when(seq < n_seq)(lambda: scores(c))
    pl.when((seq < n_seq) & last)(softmax)

    _prompt_attn_step(pl.program_id(1), q_ref, k_ref, v_ref, gap_ref, lq1, lk1, lq2, lk2,
                      subg_ref, op_ref, *prompt_scratch, tq=tq, tk=tk)


def _sample_operands(page_table, q_s, k_s, v_s, ga_s, cache_k, cache_v):
    db = q_s.shape[0]
    n_pool = cache_k.shape[0]
    ck = jnp.transpose(cache_k, (0, 2, 3, 4, 1)).reshape(n_pool, D_ATTN, CHUNK)
    cv = cache_v.reshape(n_pool, CHUNK * N_HEADS, DV)
    expand = (jnp.arange(CHUNK * N_HEADS)[None, :] // N_HEADS
              == jnp.arange(CHUNK)[:, None]).astype(BF16)
    per_seq = (q_s.reshape(db, 1, D_ATTN),
               jnp.broadcast_to(q_s[:, :, None], (db, D_ATTN, LANES)),
               k_s.reshape(db, 1, D_ATTN),
               v_s.reshape(db, N_HEADS, DV),
               ga_s.reshape(db, N_HEADS, DV))
    return page_table.reshape(-1), per_seq, expand, ck, cv


_PER_SEQ_BLOCKS = ((1, D_ATTN), (D_ATTN, LANES), (1, D_ATTN), (N_HEADS, DV), (N_HEADS, DV))


def _sample_scratch(n_chunks, span):
    return [pltpu.VMEM((n_chunks, 2 * N_HEADS, span), F32),
            pltpu.VMEM((n_chunks, 2 * N_HEADS, span), BF16),
            pltpu.VMEM((2 * N_HEADS, 1), F32),
            pltpu.VMEM((2 * N_HEADS, DV), F32)]


def _sample_attn(operands, lq1, lk1, lq2, lk2, subg, *, first, count, n_pages, pages):
    pt, per_seq, expand, ck, cv = operands
    n_chunks = n_pages // pages

    def page(b, chunk, j, pt_):
        return pt_[(first + b) * n_pages + chunk * pages + j]

    per_b = lambda shp: pl.BlockSpec((1,) + shp, lambda b, c, pt_: (first + b, 0, 0))
    kspec = lambda j: pl.BlockSpec(
        (None, D_ATTN, CHUNK),
        lambda b, c, pt_: (page(b, jnp.minimum(c, n_chunks - 1), j, pt_), 0, 0))
    vspec = lambda j: pl.BlockSpec(
        (None, CHUNK * N_HEADS, DV),
        lambda b, c, pt_: (page(b, jnp.maximum(c - n_chunks, 0), j, pt_), 0, 0))
    small = lambda n: pl.BlockSpec((1, n), lambda b, c, pt_: (0, 0))
    grid_spec = pltpu.PrefetchScalarGridSpec(
        num_scalar_prefetch=1,
        grid=(count, 2 * n_chunks),
        in_specs=[per_b(shp) for shp in _PER_SEQ_BLOCKS]
                 + [small(DH), small(DH), small(DH), small(DH), small(DV),
                    pl.BlockSpec((CHUNK, CHUNK * N_HEADS), lambda b, c, pt_: (0, 0))]
                 + [kspec(j) for j in range(pages)] + [vspec(j) for j in range(pages)],
        out_specs=pl.BlockSpec((1, N_HEADS, DV), lambda b, c, pt_: (b, 0, 0)),
        scratch_shapes=_sample_scratch(n_chunks, pages * CHUNK),
    )
    return pl.pallas_call(
        functools.partial(_sample_attn_kernel, pages=pages, n_chunks=n_chunks),
        grid_spec=grid_spec,
        out_shape=jax.ShapeDtypeStruct((count, N_HEADS, DV), BF16),
        compiler_params=pltpu.CompilerParams(
            dimension_semantics=("parallel", "arbitrary"),
            vmem_limit_bytes=48 * MIB),
        name="sample_attn",
    )(pt, *per_seq, lq1, lk1, lq2, lk2, subg, expand, *([ck] * pages), *([cv] * pages))


def _fused_attn(z, kh, vh, operands, lq1, lk1, lq2, lk2, subg, *, tq, n_pages, pages):
    pt, per_seq, expand, ck, cv = operands
    s = z.shape[0]
    tk = tq // 2
    n_q = s // tq
    n_chunks = n_pages // pages
    db = per_seq[0].shape[0]
    n_seq = min(db, N_HEADS * n_q // n_chunks - 1)

    def step(h, i):
        return h * n_q + i

    def key_page(h, i, j, pt_):
        g = jnp.minimum(step(h, i), n_seq * n_chunks - 1)
        return pt_[(g // n_chunks) * n_pages + (g % n_chunks) * pages + j]

    def value_page(h, i, j, pt_):
        g = jnp.clip(step(h, i) - n_chunks, 0, n_seq * n_chunks - 1)
        return pt_[(g // n_chunks) * n_pages + (g % n_chunks) * pages + j]

    key_seq = lambda h, i: jnp.minimum(step(h, i) // n_chunks, n_seq - 1)
    value_seq = lambda h, i: jnp.clip(step(h, i) // n_chunks - 1, 0, n_seq - 1)
    seq_of = (key_seq, key_seq, key_seq, value_seq, value_seq)
    per_b = [pl.BlockSpec((1,) + shp, lambda h, i, pt_, f=f: (f(h, i), 0, 0))
             for shp, f in zip(_PER_SEQ_BLOCKS, seq_of)]
    kspec = lambda j: pl.BlockSpec((None, D_ATTN, CHUNK),
                                   lambda h, i, pt_: (key_page(h, i, j, pt_), 0, 0))
    vspec = lambda j: pl.BlockSpec((None, CHUNK * N_HEADS, DV),
                                   lambda h, i, pt_: (value_page(h, i, j, pt_), 0, 0))
    small = lambda n: pl.BlockSpec((1, n), lambda h, i, pt_: (0, 0))
    grid_spec = pltpu.PrefetchScalarGridSpec(
        num_scalar_prefetch=1,
        grid=(N_HEADS, n_q),
        in_specs=[pl.BlockSpec((tq, LANES), lambda h, i, pt_: (i, h)),
                  pl.BlockSpec((None, s, LANES), lambda h, i, pt_: (h, 0, 0)),
                  pl.BlockSpec((None, s, LANES), lambda h, i, pt_: (h, 0, 0)),
                  pl.BlockSpec((tq, LANES), lambda h, i, pt_: (i, Z_GATE_A * N_HEADS + h)),
                  small(DH), small(DH), small(DH), small(DH), small(DV)]
                 + per_b
                 + [pl.BlockSpec((CHUNK, CHUNK * N_HEADS), lambda h, i, pt_: (0, 0))]
                 + [kspec(j) for j in range(pages)] + [vspec(j) for j in range(pages)],
        out_specs=[pl.BlockSpec((tq, LANES), lambda h, i, pt_: (i, h)),
                   pl.BlockSpec((1, N_HEADS, DV), lambda h, i, pt_: (value_seq(h, i), 0, 0))],
        scratch_shapes=_prompt_attn_scratch(s, tq, tk) + _sample_scratch(n_chunks, pages * CHUNK),
    )
    a_p, a_s = pl.pallas_call(
        functools.partial(_fused_attn_kernel, tq=tq, tk=tk, pages=pages, n_chunks=n_chunks,
                          n_seq=n_seq),
        grid_spec=grid_spec,
        out_shape=[jax.ShapeDtypeStruct((s, D_ATTN), BF16),
                   jax.ShapeDtypeStruct((n_seq, N_HEADS, DV), BF16)],
        compiler_params=pltpu.CompilerParams(
            dimension_semantics=("arbitrary", "arbitrary"),
            vmem_limit_bytes=58 * MIB),
        name="fused_attn",
    )(pt, z, kh, vh, z, lq1, lk1, lq2, lk2, subg, *per_seq, expand,
      *([ck] * pages), *([cv] * pages))
    return a_p, a_s, n_seq


def _layernorm(x, g, b):
    mu = jnp.mean(x, axis=-1, keepdims=True)
    xc = x - mu
    var = jnp.mean(xc * xc, axis=-1, keepdims=True)
    return xc * lax.rsqrt(var + EPS) * g + b


def _sgate_prompt_kernel(u_ref, vsg_ref, gb_ref, lng_ref, lnb_ref, ws_ref, bias_ref, o_ref):
    rows = u_ref.shape[0]
    n_c = rows // CHUNK
    vn = _layernorm(vsg_ref[...].astype(F32), lng_ref[...], lnb_ref[...]).astype(BF16)
    t = lax.broadcasted_iota(jnp.int32, (CHUNK, CHUNK), 0)
    s = lax.broadcasted_iota(jnp.int32, (CHUNK, CHUNK), 1)
    causal = s <= t
    for g in range(N_GROUPS):
        cols = slice(g * SG_GROUP, (g + 1) * SG_GROUP)
        wm = jnp.where(causal, ws_ref[g], 0.0).astype(BF16)
        rhs = jnp.concatenate([vn[c * CHUNK:(c + 1) * CHUNK, cols] for c in range(n_c)], axis=1)
        sp = jnp.dot(wm, rhs, preferred_element_type=F32)
        for c in range(n_c):
            rws = slice(c * CHUNK, (c + 1) * CHUNK)
            sp_c = sp[:, c * SG_GROUP:(c + 1) * SG_GROUP] + bias_ref[:, cols]
            gb = gb_ref[rws, cols].astype(F32)
            o_ref[rws, cols] = (u_ref[rws, cols].astype(F32) * sp_c * _silu(gb)).astype(o_ref.dtype)


def _sgate_sample_kernel(u_ref, vsg_ref, gb_ref, lng_ref, lnb_ref, w0_ref, b0_ref, o_ref, vn_ref):
    vn = _layernorm(vsg_ref[...], lng_ref[...], lnb_ref[...])
    vn_ref[...] = vn
    sp = w0_ref[...] * vn + b0_ref[...]
    o_ref[...] = (u_ref[...] * sp * _silu(gb_ref[...])).astype(o_ref.dtype)


def _sgate_sample(zs, ln_g, ln_b, w0, b0):
    db = zs.shape[0]
    const2 = lambda shp: pl.BlockSpec(shp, lambda i: (0, 0))
    return pl.pallas_call(
        _sgate_sample_kernel,
        grid=(1,),
        in_specs=[
            pl.BlockSpec((db, D_SG), lambda i: (0, Z_U)),
            pl.BlockSpec((db, D_SG), lambda i: (0, Z_VSG)),
            pl.BlockSpec((db, D_SG), lambda i: (0, Z_GATE_B)),
            const2((1, D_SG)), const2((1, D_SG)), const2((1, D_SG)), const2((1, D_SG)),
        ],
        out_specs=[pl.BlockSpec((db, D_SG), lambda i: (0, 0)),
                   pl.BlockSpec((db, D_SG), lambda i: (0, 0))],
        out_shape=[jax.ShapeDtypeStruct((db, D_SG), BF16),
                   jax.ShapeDtypeStruct((db, D_SG), F32)],
        name="sgate_sample",
    )(zs, zs, zs, ln_g, ln_b, w0, b0)


def _merge_kernel(x_ref, a_ref, b_ref, ga0_ref, ga1_ref, gb0_ref, gb1_ref,
                  wpa_ref, wpb_ref, wo_ref, np_ref, o_ref):
    ya = jnp.dot(a_ref[...], wpa_ref[...], preferred_element_type=F32)
    yb = jnp.dot(b_ref[...], wpb_ref[...], preferred_element_type=F32)
    half = D_MODEL // 2
    sig = lambda r: jax.nn.sigmoid(r[...].astype(F32))
    m_lo = sig(ga0_ref) * ya[:, :half] + sig(gb0_ref) * yb[:, :half]
    m_hi = sig(ga1_ref) * ya[:, half:] + sig(gb1_ref) * yb[:, half:]
    m = jnp.concatenate([m_lo, m_hi], axis=1).astype(BF16)
    mo = jnp.dot(m, wo_ref[...], preferred_element_type=F32)
    ms = jnp.mean(mo * mo, axis=-1, keepdims=True)
    o_ref[...] = x_ref[...] + mo * lax.rsqrt(ms + EPS) * np_ref[...]


def _merge(x, a_in, b_in, z, wpa_bf, wpb_bf, wo_bf, norm_post, *, tm):
    m = x.shape[0]
    resident = lambda shp: pl.BlockSpec(shp, lambda i: (0, 0), pipeline_mode=pl.Buffered(1))
    zcol = lambda jcol: pl.BlockSpec((tm, COL_BLOCK), lambda i: (i, jcol))
    return pl.pallas_call(
        _merge_kernel,
        grid=(m // tm,),
        in_specs=[
            pl.BlockSpec((tm, D_MODEL), lambda i: (i, 0)),
            pl.BlockSpec((tm, D_ATTN), lambda i: (i, 0)),
            pl.BlockSpec((tm, D_SG), lambda i: (i, 0)),
            zcol(Z_GA), zcol(Z_GA + 1), zcol(Z_GB), zcol(Z_GB + 1),
            resident((D_ATTN, D_MODEL)), resident((D_SG, D_MODEL)), resident((D_MODEL, D_MODEL)),
            resident((1, D_MODEL)),
        ],
        out_specs=pl.BlockSpec((tm, D_MODEL), lambda i: (i, 0)),
        out_shape=jax.ShapeDtypeStruct((m, D_MODEL), F32),
        compiler_params=pltpu.CompilerParams(dimension_semantics=("parallel",),
                                             vmem_limit_bytes=56 * MIB),
        name="merge",
    )(x, a_in, b_in, z, z, z, z, wpa_bf, wpb_bf, wo_bf, norm_post)


def _sgate_merge_kernel(x_ref, a_ref, u_ref, vsg_ref, gbt_ref, lng_ref, lnb_ref, ws_ref, bias_ref,
                        ga0_ref, ga1_ref, gb0_ref, gb1_ref, wpa_ref, wpb_ref, wo_ref, np_ref,
                        o_ref, b_ref):
    _sgate_prompt_kernel(u_ref, vsg_ref, gbt_ref, lng_ref, lnb_ref, ws_ref, bias_ref, b_ref)
    _merge_kernel(x_ref, a_ref, b_ref, ga0_ref, ga1_ref, gb0_ref, gb1_ref,
                  wpa_ref, wpb_ref, wo_ref, np_ref, o_ref)


def _sgate_merge(x, a_in, z, ln_g, ln_b, w_s, bias, wpa_bf, wpb_bf, wo_bf, norm_post, *, tm):
    m = x.shape[0]
    resident = lambda shp: pl.BlockSpec(shp, lambda i: (0,) * len(shp),
                                        pipeline_mode=pl.Buffered(1))
    zcol = lambda jcol: pl.BlockSpec((tm, COL_BLOCK), lambda i: (i, jcol))
    return pl.pallas_call(
        _sgate_merge_kernel,
        grid=(m // tm,),
        in_specs=[
            pl.BlockSpec((tm, D_MODEL), lambda i: (i, 0)),
            pl.BlockSpec((tm, D_ATTN), lambda i: (i, 0)),
            zcol(Z_U), zcol(Z_VSG), zcol(Z_GATE_B),
            resident((1, D_SG)), resident((1, D_SG)),
            resident((N_GROUPS, CHUNK, CHUNK)), resident((CHUNK, D_SG)),
            zcol(Z_GA), zcol(Z_GA + 1), zcol(Z_GB), zcol(Z_GB + 1),
            resident((D_ATTN, D_MODEL)), resident((D_SG, D_MODEL)), resident((D_MODEL, D_MODEL)),
            resident((1, D_MODEL)),
        ],
        out_specs=pl.BlockSpec((tm, D_MODEL), lambda i: (i, 0)),
        out_shape=jax.ShapeDtypeStruct((m, D_MODEL), F32),
        scratch_shapes=[pltpu.VMEM((tm, D_SG), BF16)],
        compiler_params=pltpu.CompilerParams(dimension_semantics=("parallel",),
                                             vmem_limit_bytes=56 * MIB),
        name="sgate_merge",
    )(x, a_in, z, z, z, ln_g, ln_b, w_s, bias, z, z, z, z, wpa_bf, wpb_bf, wo_bf, norm_post)


def _rope_tables(pos):
    half = DH // 2
    inv = ROPE_THETA ** (-(jnp.arange(half, dtype=F32) * 2.0 / DH))
    ang = pos.astype(F32)[:, None] * inv[None, :]
    cos = jnp.cos(ang)
    sin = jnp.sin(ang)
    reps = LANES // DH
    cos_t = jnp.tile(jnp.concatenate([cos, cos], axis=1), (1, reps))
    sin_t = jnp.tile(jnp.concatenate([-sin, sin], axis=1), (1, reps))
    return cos_t, sin_t


def kernel(x_prompt, x_sample, cache_k, cache_v, page_table, norm_pre, w_in, lam_q1, lam_k1,
           lam_q2, lam_k2, sub_g, w_pa, ln_g, ln_b, w_s, b_s, w_pb, w_o, norm_post):
    b, s, _ = x_prompt.shape
    db, t, _ = x_sample.shape
    assert b == 1 and t == 1 and norm_pre.shape[0] == 1
    n_past = page_table.shape[1] * cache_k.shape[2]

    wpa_bf = w_pa[0].astype(BF16)
    wpb_bf = w_pb[0].astype(BF16)
    wo_bf = w_o[0].astype(BF16)
    lams = (lam_q1, lam_k1, lam_q2, lam_k2)

    xs = x_sample.reshape(db, D_MODEL)
    cos_s, sin_s = _rope_tables(jnp.full((db,), n_past, dtype=jnp.int32))
    z_s, k_s, v_s, _, _, w_in_bf = _inproj(xs, norm_pre, w_in[0], cos_s, sin_s, tm=db,
                                           z_dtype=F32)
    sample_ops = _sample_operands(page_table, z_s[:, :D_ATTN], k_s, v_s,
                                  z_s[:, Z_GATE_A * COL_BLOCK:(Z_GATE_A + 1) * COL_BLOCK], cache_k[0], cache_v[0])
    n_pages = page_table.shape[1]

    xp = x_prompt.reshape(s, D_MODEL)
    cos_p, sin_p = _rope_tables(jnp.arange(s, dtype=jnp.int32))
    z_p, k_p, v_p, kh_p, vh_p = _inproj(xp, norm_pre, w_in_bf, cos_p, sin_p, tm=1024,
                                        z_dtype=BF16)
    a_p, a_s, n_seq = _fused_attn(z_p, kh_p, vh_p, sample_ops, *lams, sub_g, tq=512,
                                  n_pages=n_pages, pages=SAMPLE_PAGES)
    bias_p = jnp.repeat(jnp.transpose(b_s[0]), SG_GROUP, axis=1)
    y_p = _sgate_merge(xp, a_p, z_p, ln_g, ln_b, w_s[0], bias_p, wpa_bf, wpb_bf, wo_bf,
                       norm_post, tm=256)

    if n_seq < db:
        a_rest = _sample_attn(sample_ops, *lams, sub_g, first=n_seq, count=db - n_seq,
                              n_pages=n_pages, pages=SAMPLE_PAGES)
        a_s = jnp.concatenate([a_s, a_rest], axis=0)
    a_s = a_s.reshape(db, D_ATTN)
    w0 = jnp.repeat(w_s[0, :, 0, 0], SG_GROUP)[None, :]
    b0 = jnp.repeat(b_s[0, :, 0], SG_GROUP)[None, :]
    b_sm, vn_s = _sgate_sample(z_s, ln_g, ln_b, w0, b0)
    y_s = _merge(xs, a_s, b_sm, z_s, wpa_bf, wpb_bf, wo_bf, norm_post, tm=db)

    return (y_p.reshape(b, s, D_MODEL),
            y_s.reshape(db, t, D_MODEL),
            k_p.reshape(1, b, s, N_HEADS, 2, DH),
            v_p.reshape(1, b, s, N_HEADS, DV),
            k_s.reshape(1, db, t, N_HEADS, 2, DH),
            v_s.reshape(1, db, t, N_HEADS, DV),
            vn_s.reshape(1, db, t, D_SG))
```

```python
import functools
import math

import jax
import jax.numpy as jnp
from jax import lax
from jax.experimental import pallas as pl
from jax.experimental.pallas import tpu as pltpu

F32 = jnp.float32
BF16 = jnp.bfloat16

D_MODEL = 2048
D_ATTN = D_MODEL // 2
N_HEADS = 8
DH = D_ATTN // (2 * N_HEADS)
DV = 2 * DH
D_SG = D_MODEL // 2
N_GROUPS = 8
SG_GROUP = D_SG // N_GROUPS
CHUNK = 128
ROPE_THETA = 10000.0
EPS = 1e-6
NEG = -1e30
D_IN = 4 * D_ATTN + 3 * D_SG + 2 * D_MODEL
COL_BLOCK = 1024
N_COL_BLOCKS = D_IN // COL_BLOCK
Z_Q, Z_GATE_A, Z_U, Z_VSG, Z_GATE_B, Z_GA, Z_GB = 0, 1, 2, 3, 4, 5, 7
Z_WIDTH = D_IN - 2 * D_ATTN
LANES = 128
Q_SCALE = DH ** -0.5 * math.log2(math.e)
LAM_INIT = 0.8 - 0.6 * math.exp(-0.3 * 0)
MIB = 1024 * 1024
ONES_ROWS = 16
SAMPLE_PAGES = 16


def _silu(x):
    return x * jax.nn.sigmoid(x)


def _lam(lq1, lk1, lq2, lk2):
    a = jnp.sum(lq1[...] * lk1[...], axis=-1, keepdims=True)
    b = jnp.sum(lq2[...] * lk2[...], axis=-1, keepdims=True)
    return jnp.exp(a) - jnp.exp(b) + LAM_INIT


def _inproj_kernel(x_ref, g_ref, w_ref, cos_ref, sin_ref, z_ref, k_ref, v_ref, kh_ref, vh_ref,
                   *rest):
    xn_ref = rest[-1]

    def by_head(dst, a):
        for h in range(N_HEADS):
            dst[h] = a[:, h * LANES:(h + 1) * LANES].astype(dst.dtype)

    j = pl.program_id(1)
    tm = x_ref.shape[0]

    @pl.when(j == 0)
    def _():
        x = x_ref[...]
        ms = jnp.mean(x * x, axis=-1, keepdims=True)
        xn_ref[...] = (x * lax.rsqrt(ms + EPS) * g_ref[...]).astype(BF16)

    w = w_ref[...].astype(BF16)
    if len(rest) == 2:
        rest[0][...] = w
    acc = jnp.dot(xn_ref[...], w, preferred_element_type=F32)

    def rope(a):
        cos = cos_ref[...]
        sin = sin_ref[...]
        lane = lax.broadcasted_iota(jnp.int32, (tm, LANES), 1)
        first_half = (lane % DH) < (DH // 2)
        outs = []
        for c in range(COL_BLOCK // LANES):
            blk = a[:, c * LANES:(c + 1) * LANES]
            partner = jnp.where(first_half,
                                pltpu.roll(blk, LANES - DH // 2, 1),
                                pltpu.roll(blk, DH // 2, 1))
            outs.append(blk * cos + partner * sin)
        return jnp.concatenate(outs, axis=1)

    @pl.when(j == 0)
    def _():
        z_ref[...] = (rope(acc) * Q_SCALE).astype(z_ref.dtype)

    @pl.when(j == 1)
    def _():
        r = rope(acc)
        k_ref[...] = r
        by_head(kh_ref, r)

    @pl.when(j == 2)
    def _():
        v_ref[...] = acc
        by_head(vh_ref, acc)

    @pl.when(j > 2)
    def _():
        z_ref[...] = acc.astype(z_ref.dtype)


def _inproj(x, g, w_bf, cos_t, sin_t, *, tm, z_dtype):
    m = x.shape[0]
    emit_w = w_bf.dtype != BF16
    assert not emit_w or m == tm
    w_out_spec = [pl.BlockSpec((D_MODEL, COL_BLOCK), lambda i, j: (0, j))] if emit_w else []
    w_out_shape = [jax.ShapeDtypeStruct((D_MODEL, D_IN), BF16)] if emit_w else []
    return pl.pallas_call(
        _inproj_kernel,
        grid=(m // tm, N_COL_BLOCKS),
        in_specs=[
            pl.BlockSpec((tm, D_MODEL), lambda i, j: (i, 0)),
            pl.BlockSpec((1, D_MODEL), lambda i, j: (0, 0)),
            pl.BlockSpec((D_MODEL, COL_BLOCK), lambda i, j: (0, j)),
            pl.BlockSpec((tm, LANES), lambda i, j: (i, 0)),
            pl.BlockSpec((tm, LANES), lambda i, j: (i, 0)),
        ],
        out_specs=[
            pl.BlockSpec((tm, COL_BLOCK), lambda i, j: (i, jnp.maximum(j - 2, 0))),
            pl.BlockSpec((tm, COL_BLOCK), lambda i, j: (i, 0), pipeline_mode=pl.Buffered(1)),
            pl.BlockSpec((tm, COL_BLOCK), lambda i, j: (i, 0), pipeline_mode=pl.Buffered(1)),
            pl.BlockSpec((N_HEADS, tm, LANES), lambda i, j: (0, i, 0), pipeline_mode=pl.Buffered(1)),
            pl.BlockSpec((N_HEADS, tm, LANES), lambda i, j: (0, i, 0), pipeline_mode=pl.Buffered(1)),
        ] + w_out_spec,
        out_shape=[
            jax.ShapeDtypeStruct((m, Z_WIDTH), z_dtype),
            jax.ShapeDtypeStruct((m, D_ATTN), F32),
            jax.ShapeDtypeStruct((m, D_ATTN), F32),
            jax.ShapeDtypeStruct((N_HEADS, m, LANES), BF16),
            jax.ShapeDtypeStruct((N_HEADS, m, LANES), BF16),
        ] + w_out_shape,
        scratch_shapes=[pltpu.VMEM((tm, D_MODEL), BF16)],
        compiler_params=pltpu.CompilerParams(
            dimension_semantics=("parallel", "arbitrary"),
            vmem_limit_bytes=60 * MIB),
        name="inproj",
    )(x, g, w_bf, cos_t, sin_t)


def _prompt_attn_step(qi, q_ref, k_ref, v_ref, ga_ref, lq1, lk1, lq2, lk2, subg_ref, o_ref,
                      vt_ref, qq_ref, sa_ref, sb_ref, m_ref, acc_ref, *, tq, tk):
    n_kv = k_ref.shape[0] // tk

    @pl.when(qi == 0)
    def _():
        def transpose_block(kb, carry):
            start = pl.multiple_of(kb * tk, tk)
            vt_ref[kb, :DV, :] = v_ref[pl.ds(start, tk), :].astype(F32).T.astype(BF16)
            vt_ref[kb, DV:, :] = jnp.ones((ONES_ROWS, tk), BF16)
            return carry
        lax.fori_loop(0, n_kv, transpose_block, 0)

    q_t = q_ref[...].astype(F32).T
    feat = lax.broadcasted_iota(jnp.int32, (LANES, tq), 0)
    qq_ref[:, :tq] = jnp.where(feat < DH, q_t, 0.0).astype(BF16)
    qq_ref[:, tq:] = jnp.where(feat >= DH, q_t, 0.0).astype(BF16)

    m_ref[...] = jnp.full(m_ref.shape, NEG, F32)
    acc_ref[...] = jnp.zeros(acc_ref.shape, F32)

    def scores(kb, dst):
        start = pl.multiple_of(kb * tk, tk)
        dst[...] = jnp.dot(k_ref[pl.ds(start, tk), :], qq_ref[...],
                           preferred_element_type=F32)

    def consume(src, kb, masked):
        s_t = src[...]
        if masked:
            key = kb * tk + lax.broadcasted_iota(jnp.int32, (tk, 2 * tq), 0)
            qry = lax.broadcasted_iota(jnp.int32, (tk, 2 * tq), 1)
            qry = qi * tq + jnp.where(qry >= tq, qry - tq, qry)
            s_t = jnp.where(key <= qry, s_t, NEG)
        m_old = m_ref[...]
        m_new = jnp.maximum(m_old, jnp.max(s_t, axis=0, keepdims=True))
        alpha = jnp.exp2(m_old - m_new)
        p_t = jnp.exp2(s_t - m_new)
        acc_ref[...] = alpha * acc_ref[...] + jnp.dot(vt_ref[kb], p_t.astype(BF16),
                                                      preferred_element_type=F32)
        m_ref[...] = m_new

    assert tq == 2 * tk
    scores(0, sa_ref)

    def pair(first):
        scores(first + 1, sb_ref)
        consume(sa_ref, first, False)
        scores(first + 2, sa_ref)
        consume(sb_ref, first + 1, False)

    def four_pairs(p, carry):
        for r in range(4):
            pair(8 * p + 2 * r)
        return carry

    lax.fori_loop(0, qi // 4, four_pairs, 0)
    left = 8 * (qi // 4)

    @pl.when(qi % 4 >= 2)
    def _():
        pair(left)
        pair(left + 2)

    @pl.when(qi % 2 == 1)
    def _():
        pair(2 * qi - 2)

    scores(2 * qi + 1, sb_ref)
    consume(sa_ref, 2 * qi, True)
    consume(sb_ref, 2 * qi + 1, True)

    lam = _lam(lq1, lk1, lq2, lk2)
    o_all = acc_ref[:DV, :] / acc_ref[DV:DV + 1, :]
    o = (o_all[:, :tq] - lam * o_all[:, tq:]).T
    ms = jnp.mean(o * o, axis=-1, keepdims=True)
    on = o * lax.rsqrt(ms + EPS) * subg_ref[...] * (1.0 - LAM_INIT)
    ga = ga_ref[...].astype(F32)
    o_ref[...] = (on * _silu(ga)).astype(o_ref.dtype)


def _prompt_attn_scratch(s, tq, tk):
    return [pltpu.VMEM((s // tk, DV + ONES_ROWS, tk), BF16),
            pltpu.VMEM((LANES, 2 * tq), BF16),
            pltpu.VMEM((tk, 2 * tq), F32), pltpu.VMEM((tk, 2 * tq), F32),
            pltpu.VMEM((1, 2 * tq), F32),
            pltpu.VMEM((DV + ONES_ROWS, 2 * tq), F32)]


def _sample_phases(qrow_ref, qcol_ref, kn_ref, vn_ref, ga_ref, lq1, lk1, lq2, lk2, subg_ref,
                   expand_ref, k_pages, v_pages, o_ref, sc_ref, p_ref, pn_ref, acc_ref):
    pages = len(k_pages)
    n_maps = 2 * N_HEADS

    def scores(c):
        for h in range(N_HEADS):
            rows = slice(h * 2 * DH, (h + 1) * 2 * DH)
            q_h = qcol_ref[0, rows, :]
            for j, kp in enumerate(k_pages):
                prod = kp[rows, :] * q_h
                lanes = slice(j * CHUNK, (j + 1) * CHUNK)
                sc_ref[c, h:h + 1, lanes] = jnp.sum(prod[:DH, :], axis=0, keepdims=True)
                sc_ref[c, N_HEADS + h:N_HEADS + h + 1, lanes] = jnp.sum(
                    prod[DH:, :], axis=0, keepdims=True)

    def softmax():
        s = sc_ref[...]
        row = lax.broadcasted_iota(jnp.int32, (n_maps, D_ATTN), 0)
        seg = lax.broadcasted_iota(jnp.int32, (n_maps, D_ATTN), 1) // DH
        want = jnp.where(row < N_HEADS, 2 * row, 2 * (row - N_HEADS) + 1)
        qk_new = jnp.broadcast_to(qrow_ref[0] * kn_ref[0], (n_maps, D_ATTN))
        s_new = jnp.sum(jnp.where(seg == want, qk_new, 0.0), axis=-1, keepdims=True)
        m = jnp.max(jnp.max(s, axis=-1, keepdims=True), axis=0)
        m = jnp.maximum(m, s_new)
        p = jnp.exp2(s - m[None])
        pn = jnp.exp2(s_new - m)
        l = jnp.sum(jnp.sum(p, axis=-1, keepdims=True), axis=0) + pn
        inv_l = 1.0 / l
        p_ref[...] = (p * inv_l[None]).astype(BF16)
        pn_ref[...] = pn * inv_l
        acc_ref[...] = jnp.zeros(acc_ref.shape, F32)

    def values(c):
        pstack = jnp.concatenate(
            [p_ref[c, :, j * CHUNK:(j + 1) * CHUNK] for j in range(pages)], axis=0)
        pexp = jnp.dot(pstack, expand_ref[...], preferred_element_type=F32)
        row = lax.broadcasted_iota(jnp.int32, pexp.shape, 0)
        col = lax.broadcasted_iota(jnp.int32, pexp.shape, 1)
        pexp = jnp.where(row % N_HEADS == col % N_HEADS, pexp, 0.0).astype(BF16)
        acc = acc_ref[...]
        for j, vp in enumerate(v_pages):
            acc += jnp.dot(pexp[j * n_maps:(j + 1) * n_maps, :], vp[...].astype(BF16),
                           preferred_element_type=F32)
        acc_ref[...] = acc

    def finish():
        lam = _lam(lq1, lk1, lq2, lk2)
        v_new = vn_ref[0]
        full = acc_ref[...] + pn_ref[...] * jnp.concatenate([v_new, v_new], axis=0)
        o = full[:N_HEADS, :] - lam * full[N_HEADS:, :]
        ms = jnp.mean(o * o, axis=-1, keepdims=True)
        on = o * lax.rsqrt(ms + EPS) * subg_ref[...] * (1.0 - LAM_INIT)
        o_ref[0] = (on * _silu(ga_ref[0])).astype(o_ref.dtype)

    return scores, softmax, values, finish


def _sample_attn_kernel(pt_ref, qrow_ref, qcol_ref, kn_ref, vn_ref, ga_ref, lq1, lk1, lq2, lk2,
                        subg_ref, expand_ref, *rest, pages, n_chunks):
    del pt_ref
    scores, softmax, values, finish = _sample_phases(
        qrow_ref, qcol_ref, kn_ref, vn_ref, ga_ref, lq1, lk1, lq2, lk2, subg_ref, expand_ref,
        rest[:pages], rest[pages:2 * pages], *rest[2 * pages:])
    c = pl.program_id(1)
    pl.when(c < n_chunks)(lambda: scores(c))
    pl.when(c == n_chunks - 1)(softmax)
    pl.when(c >= n_chunks)(lambda: values(c - n_chunks))
    pl.when(c == 2 * n_chunks - 1)(finish)


def _fused_attn_kernel(pt_ref, q_ref, k_ref, v_ref, gap_ref, lq1, lk1, lq2, lk2, subg_ref,
                       qrow_ref, qcol_ref, kn_ref, vn_ref, gas_ref, expand_ref, *rest,
                       tq, tk, pages, n_chunks, n_seq):
    del pt_ref
    k_pages, v_pages = rest[:pages], rest[pages:2 * pages]
    op_ref, os_ref = rest[2 * pages:2 * pages + 2]
    prompt_scratch = rest[2 * pages + 2:2 * pages + 8]
    sample_scratch = rest[2 * pages + 8:]
    g = pl.program_id(0) * pl.num_programs(1) + pl.program_id(1)
    seq = g // n_chunks
    c = g % n_chunks
    scores, softmax, values, finish = _sample_phases(
        qrow_ref, qcol_ref, kn_ref, vn_ref, gas_ref, lq1, lk1, lq2, lk2, subg_ref, expand_ref,
        k_pages, v_pages, os_ref, *sample_scratch)
    last = c == n_chunks - 1
    trailing = (seq >= 1) & (seq <= n_seq)
    pl.when(trailing)(lambda: values(c))
    pl.when(trailing & last)(finish)
    pl.when(seq < n_seq)(lambda: scores(c))
    pl.when((seq < n_seq) & last)(softmax)

    _prompt_attn_step(pl.program_id(1), q_ref, k_ref, v_ref, gap_ref, lq1, lk1, lq2, lk2,
                      subg_ref, op_ref, *prompt_scratch, tq=tq, tk=tk)


def _sample_operands(page_table, q_s, k_s, v_s, ga_s, cache_k, cache_v):
    db = q_s.shape[0]
    n_pool = cache_k.shape[0]
    ck = jnp.transpose(cache_k, (0, 2, 3, 4, 1)).reshape(n_pool, D_ATTN, CHUNK)
    cv = cache_v.reshape(n_pool, CHUNK * N_HEADS, DV)
    expand = (jnp.arange(CHUNK * N_HEADS)[None, :] // N_HEADS
              == jnp.arange(CHUNK)[:, None]).astype(BF16)
    per_seq = (q_s.reshape(db, 1, D_ATTN),
               jnp.broadcast_to(q_s[:, :, None], (db, D_ATTN, LANES)),
               k_s.reshape(db, 1, D_ATTN),
               v_s.reshape(db, N_HEADS, DV),
               ga_s.reshape(db, N_HEADS, DV))
    return page_table.reshape(-1), per_seq, expand, ck, cv


_PER_SEQ_BLOCKS = ((1, D_ATTN), (D_ATTN, LANES), (1, D_ATTN), (N_HEADS, DV), (N_HEADS, DV))


def _sample_scratch(n_chunks, span):
    return [pltpu.VMEM((n_chunks, 2 * N_HEADS, span), F32),
            pltpu.VMEM((n_chunks, 2 * N_HEADS, span), BF16),
            pltpu.VMEM((2 * N_HEADS, 1), F32),
            pltpu.VMEM((2 * N_HEADS, DV), F32)]


def _sample_attn(operands, lq1, lk1, lq2, lk2, subg, *, first, count, n_pages, pages):
    pt, per_seq, expand, ck, cv = operands
    n_chunks = n_pages // pages

    def page(b, chunk, j, pt_):
        return pt_[(first + b) * n_pages + chunk * pages + j]

    per_b = lambda shp: pl.BlockSpec((1,) + shp, lambda b, c, pt_: (first + b, 0, 0))
    kspec = lambda j: pl.BlockSpec(
        (None, D_ATTN, CHUNK),
        lambda b, c, pt_: (page(b, jnp.minimum(c, n_chunks - 1), j, pt_), 0, 0))
    vspec = lambda j: pl.BlockSpec(
        (None, CHUNK * N_HEADS, DV),
        lambda b, c, pt_: (page(b, jnp.maximum(c - n_chunks, 0), j, pt_), 0, 0))
    small = lambda n: pl.BlockSpec((1, n), lambda b, c, pt_: (0, 0))
    grid_spec = pltpu.PrefetchScalarGridSpec(
        num_scalar_prefetch=1,
        grid=(count, 2 * n_chunks),
        in_specs=[per_b(shp) for shp in _PER_SEQ_BLOCKS]
                 + [small(DH), small(DH), small(DH), small(DH), small(DV),
                    pl.BlockSpec((CHUNK, CHUNK * N_HEADS), lambda b, c, pt_: (0, 0))]
                 + [kspec(j) for j in range(pages)] + [vspec(j) for j in range(pages)],
        out_specs=pl.BlockSpec((1, N_HEADS, DV), lambda b, c, pt_: (b, 0, 0)),
        scratch_shapes=_sample_scratch(n_chunks, pages * CHUNK),
    )
    return pl.pallas_call(
        functools.partial(_sample_attn_kernel, pages=pages, n_chunks=n_chunks),
        grid_spec=grid_spec,
        out_shape=jax.ShapeDtypeStruct((count, N_HEADS, DV), BF16),
        compiler_params=pltpu.CompilerParams(
            dimension_semantics=("parallel", "arbitrary"),
            vmem_limit_bytes=48 * MIB),
        name="sample_attn",
    )(pt, *per_seq, lq1, lk1, lq2, lk2, subg, expand, *([ck] * pages), *([cv] * pages))


def _fused_attn(z, kh, vh, operands, lq1, lk1, lq2, lk2, subg, *, tq, n_pages, pages):
    pt, per_seq, expand, ck, cv = operands
    s = z.shape[0]
    tk = tq // 2
    n_q = s // tq
    n_chunks = n_pages // pages
    db = per_seq[0].shape[0]
    n_seq = min(db, N_HEADS * n_q // n_chunks - 1)

    def step(h, i):
        return h * n_q + i

    def key_page(h, i, j, pt_):
        g = jnp.minimum(step(h, i), n_seq * n_chunks - 1)
        return pt_[(g // n_chunks) * n_pages + (g % n_chunks) * pages + j]

    def value_page(h, i, j, pt_):
        g = jnp.clip(step(h, i) - n_chunks, 0, n_seq * n_chunks - 1)
        return pt_[(g // n_chunks) * n_pages + (g % n_chunks) * pages + j]

    key_seq = lambda h, i: jnp.minimum(step(h, i) // n_chunks, n_seq - 1)
    value_seq = lambda h, i: jnp.clip(step(h, i) // n_chunks - 1, 0, n_seq - 1)
    seq_of = (key_seq, key_seq, key_seq, value_seq, value_seq)
    per_b = [pl.BlockSpec((1,) + shp, lambda h, i, pt_, f=f: (f(h, i), 0, 0))
             for shp, f in zip(_PER_SEQ_BLOCKS, seq_of)]
    kspec = lambda j: pl.BlockSpec((None, D_ATTN, CHUNK),
                                   lambda h, i, pt_: (key_page(h, i, j, pt_), 0, 0))
    vspec = lambda j: pl.BlockSpec((None, CHUNK * N_HEADS, DV),
                                   lambda h, i, pt_: (value_page(h, i, j, pt_), 0, 0))
    small = lambda n: pl.BlockSpec((1, n), lambda h, i, pt_: (0, 0))
    grid_spec = pltpu.PrefetchScalarGridSpec(
        num_scalar_prefetch=1,
        grid=(N_HEADS, n_q),
        in_specs=[pl.BlockSpec((tq, LANES), lambda h, i, pt_: (i, h)),
                  pl.BlockSpec((None, s, LANES), lambda h, i, pt_: (h, 0, 0)),
                  pl.BlockSpec((None, s, LANES), lambda h, i, pt_: (h, 0, 0)),
                  pl.BlockSpec((tq, LANES), lambda h, i, pt_: (i, Z_GATE_A * N_HEADS + h)),
                  small(DH), small(DH), small(DH), small(DH), small(DV)]
                 + per_b
                 + [pl.BlockSpec((CHUNK, CHUNK * N_HEADS), lambda h, i, pt_: (0, 0))]
                 + [kspec(j) for j in range(pages)] + [vspec(j) for j in range(pages)],
        out_specs=[pl.BlockSpec((tq, LANES), lambda h, i, pt_: (i, h)),
                   pl.BlockSpec((1, N_HEADS, DV), lambda h, i, pt_: (value_seq(h, i), 0, 0))],
        scratch_shapes=_prompt_attn_scratch(s, tq, tk) + _sample_scratch(n_chunks, pages * CHUNK),
    )
    a_p, a_s = pl.pallas_call(
        functools.partial(_fused_attn_kernel, tq=tq, tk=tk, pages=pages, n_chunks=n_chunks,
                          n_seq=n_seq),
        grid_spec=grid_spec,
        out_shape=[jax.ShapeDtypeStruct((s, D_ATTN), BF16),
                   jax.ShapeDtypeStruct((n_seq, N_HEADS, DV), BF16)],
        compiler_params=pltpu.CompilerParams(
            dimension_semantics=("arbitrary", "arbitrary"),
            vmem_limit_bytes=58 * MIB),
        name="fused_attn",
    )(pt, z, kh, vh, z, lq1, lk1, lq2, lk2, subg, *per_seq, expand,
      *([ck] * pages), *([cv] * pages))
    return a_p, a_s, n_seq


def _layernorm(x, g, b):
    mu = jnp.mean(x, axis=-1, keepdims=True)
    xc = x - mu
    var = jnp.mean(xc * xc, axis=-1, keepdims=True)
    return xc * lax.rsqrt(var + EPS) * g + b


def _sgate_prompt_kernel(u_ref, vsg_ref, gb_ref, lng_ref, lnb_ref, ws_ref, bias_ref, o_ref):
    rows = u_ref.shape[0]
    n_c = rows // CHUNK
    vn = _layernorm(vsg_ref[...].astype(F32), lng_ref[...], lnb_ref[...]).astype(BF16)
    t = lax.broadcasted_iota(jnp.int32, (CHUNK, CHUNK), 0)
    s = lax.broadcasted_iota(jnp.int32, (CHUNK, CHUNK), 1)
    causal = s <= t
    for g in range(N_GROUPS):
        cols = slice(g * SG_GROUP, (g + 1) * SG_GROUP)
        wm = jnp.where(causal, ws_ref[g], 0.0).astype(BF16)
        rhs = jnp.concatenate([vn[c * CHUNK:(c + 1) * CHUNK, cols] for c in range(n_c)], axis=1)
        sp = jnp.dot(wm, rhs, preferred_element_type=F32)
        for c in range(n_c):
            rws = slice(c * CHUNK, (c + 1) * CHUNK)
            sp_c = sp[:, c * SG_GROUP:(c + 1) * SG_GROUP] + bias_ref[:, cols]
            gb = gb_ref[rws, cols].astype(F32)
            o_ref[rws, cols] = (u_ref[rws, cols].astype(F32) * sp_c * _silu(gb)).astype(o_ref.dtype)


def _sgate_sample_kernel(u_ref, vsg_ref, gb_ref, lng_ref, lnb_ref, w0_ref, b0_ref, o_ref, vn_ref):
    vn = _layernorm(vsg_ref[...], lng_ref[...], lnb_ref[...])
    vn_ref[...] = vn
    sp = w0_ref[...] * vn + b0_ref[...]
    o_ref[...] = (u_ref[...] * sp * _silu(gb_ref[...])).astype(o_ref.dtype)


def _sgate_sample(zs, ln_g, ln_b, w0, b0):
    db = zs.shape[0]
    const2 = lambda shp: pl.BlockSpec(shp, lambda i: (0, 0))
    return pl.pallas_call(
        _sgate_sample_kernel,
        grid=(1,),
        in_specs=[
            pl.BlockSpec((db, D_SG), lambda i: (0, Z_U)),
            pl.BlockSpec((db, D_SG), lambda i: (0, Z_VSG)),
            pl.BlockSpec((db, D_SG), lambda i: (0, Z_GATE_B)),
            const2((1, D_SG)), const2((1, D_SG)), const2((1, D_SG)), const2((1, D_SG)),
        ],
        out_specs=[pl.BlockSpec((db, D_SG), lambda i: (0, 0)),
                   pl.BlockSpec((db, D_SG), lambda i: (0, 0))],
        out_shape=[jax.ShapeDtypeStruct((db, D_SG), BF16),
                   jax.ShapeDtypeStruct((db, D_SG), F32)],
        name="sgate_sample",
    )(zs, zs, zs, ln_g, ln_b, w0, b0)


def _merge_kernel(x_ref, a_ref, b_ref, ga0_ref, ga1_ref, gb0_ref, gb1_ref,
                  wpa_ref, wpb_ref, wo_ref, np_ref, o_ref):
    ya = jnp.dot(a_ref[...], wpa_ref[...], preferred_element_type=F32)
    yb = jnp.dot(b_ref[...], wpb_ref[...], preferred_element_type=F32)
    half = D_MODEL // 2
    sig = lambda r: jax.nn.sigmoid(r[...].astype(F32))
    m_lo = sig(ga0_ref) * ya[:, :half] + sig(gb0_ref) * yb[:, :half]
    m_hi = sig(ga1_ref) * ya[:, half:] + sig(gb1_ref) * yb[:, half:]
    m = jnp.concatenate([m_lo, m_hi], axis=1).astype(BF16)
    mo = jnp.dot(m, wo_ref[...], preferred_element_type=F32)
    ms = jnp.mean(mo * mo, axis=-1, keepdims=True)
    o_ref[...] = x_ref[...] + mo * lax.rsqrt(ms + EPS) * np_ref[...]


def _merge(x, a_in, b_in, z, wpa_bf, wpb_bf, wo_bf, norm_post, *, tm):
    m = x.shape[0]
    resident = lambda shp: pl.BlockSpec(shp, lambda i: (0, 0), pipeline_mode=pl.Buffered(1))
    zcol = lambda jcol: pl.BlockSpec((tm, COL_BLOCK), lambda i: (i, jcol))
    return pl.pallas_call(
        _merge_kernel,
        grid=(m // tm,),
        in_specs=[
            pl.BlockSpec((tm, D_MODEL), lambda i: (i, 0)),
            pl.BlockSpec((tm, D_ATTN), lambda i: (i, 0)),
            pl.BlockSpec((tm, D_SG), lambda i: (i, 0)),
            zcol(Z_GA), zcol(Z_GA + 1), zcol(Z_GB), zcol(Z_GB + 1),
            resident((D_ATTN, D_MODEL)), resident((D_SG, D_MODEL)), resident((D_MODEL, D_MODEL)),
            resident((1, D_MODEL)),
        ],
        out_specs=pl.BlockSpec((tm, D_MODEL), lambda i: (i, 0)),
        out_shape=jax.ShapeDtypeStruct((m, D_MODEL), F32),
        compiler_params=pltpu.CompilerParams(dimension_semantics=("parallel",),
                                             vmem_limit_bytes=56 * MIB),
        name="merge",
    )(x, a_in, b_in, z, z, z, z, wpa_bf, wpb_bf, wo_bf, norm_post)


def _sgate_merge_kernel(x_ref, a_ref, u_ref, vsg_ref, gbt_ref, lng_ref, lnb_ref, ws_ref, bias_ref,
                        ga0_ref, ga1_ref, gb0_ref, gb1_ref, wpa_ref, wpb_ref, wo_ref, np_ref,
                        o_ref, b_ref):
    _sgate_prompt_kernel(u_ref, vsg_ref, gbt_ref, lng_ref, lnb_ref, ws_ref, bias_ref, b_ref)
    _merge_kernel(x_ref, a_ref, b_ref, ga0_ref, ga1_ref, gb0_ref, gb1_ref,
                  wpa_ref, wpb_ref, wo_ref, np_ref, o_ref)


def _sgate_merge(x, a_in, z, ln_g, ln_b, w_s, bias, wpa_bf, wpb_bf, wo_bf, norm_post, *, tm):
    m = x.shape[0]
    resident = lambda shp: pl.BlockSpec(shp, lambda i: (0,) * len(shp),
                                        pipeline_mode=pl.Buffered(1))
    zcol = lambda jcol: pl.BlockSpec((tm, COL_BLOCK), lambda i: (i, jcol))
    return pl.pallas_call(
        _sgate_merge_kernel,
        grid=(m // tm,),
        in_specs=[
            pl.BlockSpec((tm, D_MODEL), lambda i: (i, 0)),
            pl.BlockSpec((tm, D_ATTN), lambda i: (i, 0)),
            zcol(Z_U), zcol(Z_VSG), zcol(Z_GATE_B),
            resident((1, D_SG)), resident((1, D_SG)),
            resident((N_GROUPS, CHUNK, CHUNK)), resident((CHUNK, D_SG)),
            zcol(Z_GA), zcol(Z_GA + 1), zcol(Z_GB), zcol(Z_GB + 1),
            resident((D_ATTN, D_MODEL)), resident((D_SG, D_MODEL)), resident((D_MODEL, D_MODEL)),
            resident((1, D_MODEL)),
        ],
        out_specs=pl.BlockSpec((tm, D_MODEL), lambda i: (i, 0)),
        out_shape=jax.ShapeDtypeStruct((m, D_MODEL), F32),
        scratch_shapes=[pltpu.VMEM((tm, D_SG), BF16)],
        compiler_params=pltpu.CompilerParams(dimension_semantics=("parallel",),
                                             vmem_limit_bytes=56 * MIB),
        name="sgate_merge",
    )(x, a_in, z, z, z, ln_g, ln_b, w_s, bias, z, z, z, z, wpa_bf, wpb_bf, wo_bf, norm_post)


def _rope_tables(pos):
    half = DH // 2
    inv = ROPE_THETA ** (-(jnp.arange(half, dtype=F32) * 2.0 / DH))
    ang = pos.astype(F32)[:, None] * inv[None, :]
    cos = jnp.cos(ang)
    sin = jnp.sin(ang)
    reps = LANES // DH
    cos_t = jnp.tile(jnp.concatenate([cos, cos], axis=1), (1, reps))
    sin_t = jnp.tile(jnp.concatenate([-sin, sin], axis=1), (1, reps))
    return cos_t, sin_t


def kernel(x_prompt, x_sample, cache_k, cache_v, page_table, norm_pre, w_in, lam_q1, lam_k1,
           lam_q2, lam_k2, sub_g, w_pa, ln_g, ln_b, w_s, b_s, w_pb, w_o, norm_post):
    b, s, _ = x_prompt.shape
    db, t, _ = x_sample.shape
    assert b == 1 and t == 1 and norm_pre.shape[0] == 1
    n_past = page_table.shape[1] * cache_k.shape[2]

    wpa_bf = w_pa[0].astype(BF16)
    wpb_bf = w_pb[0].astype(BF16)
    wo_bf = w_o[0].astype(BF16)
    lams = (lam_q1, lam_k1, lam_q2, lam_k2)

    xs = x_sample.reshape(db, D_MODEL)
    cos_s, sin_s = _rope_tables(jnp.full((db,), n_past, dtype=jnp.int32))
    z_s, k_s, v_s, _, _, w_in_bf = _inproj(xs, norm_pre, w_in[0], cos_s, sin_s, tm=db,
                                           z_dtype=F32)
    sample_ops = _sample_operands(page_table, z_s[:, :D_ATTN], k_s, v_s,
                                  z_s[:, Z_GATE_A * COL_BLOCK:(Z_GATE_A + 1) * COL_BLOCK], cache_k[0], cache_v[0])
    n_pages = page_table.shape[1]

    xp = x_prompt.reshape(s, D_MODEL)
    cos_p, sin_p = _rope_tables(jnp.arange(s, dtype=jnp.int32))
    z_p, k_p, v_p, kh_p, vh_p = _inproj(xp, norm_pre, w_in_bf, cos_p, sin_p, tm=1024,
                                        z_dtype=BF16)
    a_p, a_s, n_seq = _fused_attn(z_p, kh_p, vh_p, sample_ops, *lams, sub_g, tq=512,
                                  n_pages=n_pages, pages=SAMPLE_PAGES)
    bias_p = jnp.repeat(jnp.transpose(b_s[0]), SG_GROUP, axis=1)
    y_p = _sgate_merge(xp, a_p, z_p, ln_g, ln_b, w_s[0], bias_p, wpa_bf, wpb_bf, wo_bf,
                       norm_post, tm=256)

    if n_seq < db:
        a_rest = _sample_attn(sample_ops, *lams, sub_g, first=n_seq, count=db - n_seq,
                              n_pages=n_pages, pages=SAMPLE_PAGES)
        a_s = jnp.concatenate([a_s, a_rest], axis=0)
    a_s = a_s.reshape(db, D_ATTN)
    w0 = jnp.repeat(w_s[0, :, 0, 0], SG_GROUP)[None, :]
    b0 = jnp.repeat(b_s[0, :, 0], SG_GROUP)[None, :]
    b_sm, vn_s = _sgate_sample(z_s, ln_g, ln_b, w0, b0)
    y_s = _merge(xs, a_s, b_sm, z_s, wpa_bf, wpb_bf, wo_bf, norm_post, tm=db)

    return (y_p.reshape(b, s, D_MODEL),
            y_s.reshape(db, t, D_MODEL),
            k_p.reshape(1, b, s, N_HEADS, 2, DH),
            v_p.reshape(1, b, s, N_HEADS, DV),
            k_s.reshape(1, db, t, N_HEADS, 2, DH),
            v_s.reshape(1, db, t, N_HEADS, DV),
            vn_s.reshape(1, db, t, D_SG))
```

```python
import functools
import math

import jax
import jax.numpy as jnp
from jax import lax
from jax.experimental import pallas as pl
from jax.experimental.pallas import tpu as pltpu

F32 = jnp.float32
BF16 = jnp.bfloat16

D_MODEL = 2048
D_ATTN = D_MODEL // 2
N_HEADS = 8
DH = D_ATTN // (2 * N_HEADS)
DV = 2 * DH
D_SG = D_MODEL // 2
N_GROUPS = 8
SG_GROUP = D_SG // N_GROUPS
CHUNK = 128
ROPE_THETA = 10000.0
EPS = 1e-6
NEG = -1e30
D_IN = 4 * D_ATTN + 3 * D_SG + 2 * D_MODEL
COL_BLOCK = 1024
N_COL_BLOCKS = D_IN // COL_BLOCK
Z_Q, Z_GATE_A, Z_U, Z_VSG, Z_GATE_B, Z_GA, Z_GB = 0, 1, 2, 3, 4, 5, 7
Z_WIDTH = D_IN - 2 * D_ATTN
LANES = 128
Q_SCALE = DH ** -0.5 * math.log2(math.e)
LAM_INIT = 0.8 - 0.6 * math.exp(-0.3 * 0)
MIB = 1024 * 1024
ONES_ROWS = 16
SAMPLE_PAGES = 16


def _silu(x):
    return x * jax.nn.sigmoid(x)


def _lam(lq1, lk1, lq2, lk2):
    a = jnp.sum(lq1[...] * lk1[...], axis=-1, keepdims=True)
    b = jnp.sum(lq2[...] * lk2[...], axis=-1, keepdims=True)
    return jnp.exp(a) - jnp.exp(b) + LAM_INIT


def _inproj_kernel(x_ref, g_ref, w_ref, cos_ref, sin_ref, z_ref, k_ref, v_ref, kh_ref, vh_ref,
                   *rest):
    xn_ref = rest[-1]

    def by_head(dst, a):
        for h in range(N_HEADS):
            dst[h] = a[:, h * LANES:(h + 1) * LANES].astype(dst.dtype)

    j = pl.program_id(1)
    tm = x_ref.shape[0]

    @pl.when(j == 0)
    def _():
        x = x_ref[...]
        ms = jnp.mean(x * x, axis=-1, keepdims=True)
        xn_ref[...] = (x * lax.rsqrt(ms + EPS) * g_ref[...]).astype(BF16)

    w = w_ref[...].astype(BF16)
    if len(rest) == 2:
        rest[0][...] = w
    acc = jnp.dot(xn_ref[...], w, preferred_element_type=F32)

    def rope(a):
        cos = cos_ref[...]
        sin = sin_ref[...]
        lane = lax.broadcasted_iota(jnp.int32, (tm, LANES), 1)
        first_half = (lane % DH) < (DH // 2)
        outs = []
        for c in range(COL_BLOCK // LANES):
            blk = a[:, c * LANES:(c + 1) * LANES]
            partner = jnp.where(first_half,
                                pltpu.roll(blk, LANES - DH // 2, 1),
                                pltpu.roll(blk, DH // 2, 1))
            outs.append(blk * cos + partner * sin)
        return jnp.concatenate(outs, axis=1)

    @pl.when(j == 0)
    def _():
        z_ref[...] = (rope(acc) * Q_SCALE).astype(z_ref.dtype)

    @pl.when(j == 1)
    def _():
        r = rope(acc)
        k_ref[...] = r
        by_head(kh_ref, r)

    @pl.when(j == 2)
    def _():
        v_ref[...] = acc
        by_head(vh_ref, acc)

    @pl.when(j > 2)
    def _():
        z_ref[...] = acc.astype(z_ref.dtype)


def _inproj(x, g, w_bf, cos_t, sin_t, *, tm, z_dtype):
    m = x.shape[0]
    emit_w = w_bf.dtype != BF16
    assert not emit_w or m == tm
    w_out_spec = [pl.BlockSpec((D_MODEL, COL_BLOCK), lambda i, j: (0, j))] if emit_w else []
    w_out_shape = [jax.ShapeDtypeStruct((D_MODEL, D_IN), BF16)] if emit_w else []
    return pl.pallas_call(
        _inproj_kernel,
        grid=(m // tm, N_COL_BLOCKS),
        in_specs=[
            pl.BlockSpec((tm, D_MODEL), lambda i, j: (i, 0)),
            pl.BlockSpec((1, D_MODEL), lambda i, j: (0, 0)),
            pl.BlockSpec((D_MODEL, COL_BLOCK), lambda i, j: (0, j)),
            pl.BlockSpec((tm, LANES), lambda i, j: (i, 0)),
            pl.BlockSpec((tm, LANES), lambda i, j: (i, 0)),
        ],
        out_specs=[
            pl.BlockSpec((tm, COL_BLOCK), lambda i, j: (i, jnp.maximum(j - 2, 0))),
            pl.BlockSpec((tm, COL_BLOCK), lambda i, j: (i, 0), pipeline_mode=pl.Buffered(1)),
            pl.BlockSpec((tm, COL_BLOCK), lambda i, j: (i, 0), pipeline_mode=pl.Buffered(1)),
            pl.BlockSpec((N_HEADS, tm, LANES), lambda i, j: (0, i, 0), pipeline_mode=pl.Buffered(1)),
            pl.BlockSpec((N_HEADS, tm, LANES), lambda i, j: (0, i, 0), pipeline_mode=pl.Buffered(1)),
        ] + w_out_spec,
        out_shape=[
            jax.ShapeDtypeStruct((m, Z_WIDTH), z_dtype),
            jax.ShapeDtypeStruct((m, D_ATTN), F32),
            jax.ShapeDtypeStruct((m, D_ATTN), F32),
            jax.ShapeDtypeStruct((N_HEADS, m, LANES), BF16),
            jax.ShapeDtypeStruct((N_HEADS, m, LANES), BF16),
        ] + w_out_shape,
        scratch_shapes=[pltpu.VMEM((tm, D_MODEL), BF16)],
        compiler_params=pltpu.CompilerParams(
            dimension_semantics=("parallel", "arbitrary"),
            vmem_limit_bytes=60 * MIB),
        name="inproj",
    )(x, g, w_bf, cos_t, sin_t)


def _prompt_attn_step(qi, q_ref, k_ref, v_ref, ga_ref, lq1, lk1, lq2, lk2, subg_ref, o_ref,
                      vt_ref, qq_ref, sa_ref, sb_ref, m_ref, acc_ref, mask_ref, *, tq, tk):
    n_kv = k_ref.shape[0] // tk

    @pl.when(qi == 0)
    def _():
        def transpose_block(kb, carry):
            start = pl.multiple_of(kb * tk, tk)
            vt_ref[kb, :DV, :] = v_ref[pl.ds(start, tk), :].astype(F32).T.astype(BF16)
            vt_ref[kb, DV:, :] = jnp.ones((ONES_ROWS, tk), BF16)
            return carry
        lax.fori_loop(0, n_kv, transpose_block, 0)
        key = lax.broadcasted_iota(jnp.int32, (tk, 2 * tq), 0)
        qry = lax.broadcasted_iota(jnp.int32, (tk, 2 * tq), 1)
        qry = jnp.where(qry >= tq, qry - tq, qry)
        for r in range(tq // tk):
            mask_ref[r] = jnp.where(key + r * tk <= qry, 0.0, NEG)

    q_t = q_ref[...].astype(F32).T
    feat = lax.broadcasted_iota(jnp.int32, (LANES, tq), 0)
    qq_ref[:, :tq] = jnp.where(feat < DH, q_t, 0.0).astype(BF16)
    qq_ref[:, tq:] = jnp.where(feat >= DH, q_t, 0.0).astype(BF16)

    m_ref[...] = jnp.full(m_ref.shape, NEG, F32)
    acc_ref[...] = jnp.zeros(acc_ref.shape, F32)

    def scores(kb, dst):
        start = pl.multiple_of(kb * tk, tk)
        dst[...] = jnp.dot(k_ref[pl.ds(start, tk), :], qq_ref[...],
                           preferred_element_type=F32)

    def consume(src, kb, diagonal=None):
        s_t = src[...]
        if diagonal is not None:
            s_t = s_t + mask_ref[diagonal]
        m_old = m_ref[...]
        m_new = jnp.maximum(m_old, jnp.max(s_t, axis=0, keepdims=True))
        alpha = jnp.exp2(m_old - m_new)
        p_t = jnp.exp2(s_t - m_new)
        acc_ref[...] = alpha * acc_ref[...] + jnp.dot(vt_ref[kb], p_t.astype(BF16),
                                                      preferred_element_type=F32)
        m_ref[...] = m_new

    assert tq == 2 * tk
    scores(0, sa_ref)

    def pair(first):
        scores(first + 1, sb_ref)
        consume(sa_ref, first)
        scores(first + 2, sa_ref)
        consume(sb_ref, first + 1)

    def four_pairs(p, carry):
        for r in range(4):
            pair(8 * p + 2 * r)
        return carry

    lax.fori_loop(0, qi // 4, four_pairs, 0)
    left = 8 * (qi // 4)

    @pl.when(qi % 4 >= 2)
    def _():
        pair(left)
        pair(left + 2)

    @pl.when(qi % 2 == 1)
    def _():
        pair(2 * qi - 2)

    scores(2 * qi + 1, sb_ref)
    consume(sa_ref, 2 * qi, diagonal=0)
    consume(sb_ref, 2 * qi + 1, diagonal=1)

    lam = _lam(lq1, lk1, lq2, lk2)
    o_all = acc_ref[:DV, :] / acc_ref[DV:DV + 1, :]
    o = (o_all[:, :tq] - lam * o_all[:, tq:]).T
    ms = jnp.mean(o * o, axis=-1, keepdims=True)
    on = o * lax.rsqrt(ms + EPS) * subg_ref[...] * (1.0 - LAM_INIT)
    ga = ga_ref[...].astype(F32)
    o_ref[...] = (on * _silu(ga)).astype(o_ref.dtype)


def _prompt_attn_scratch(s, tq, tk):
    return [pltpu.VMEM((s // tk, DV + ONES_ROWS, tk), BF16),
            pltpu.VMEM((LANES, 2 * tq), BF16),
            pltpu.VMEM((tk, 2 * tq), F32), pltpu.VMEM((tk, 2 * tq), F32),
            pltpu.VMEM((1, 2 * tq), F32),
            pltpu.VMEM((DV + ONES_ROWS, 2 * tq), F32),
            pltpu.VMEM((tq // tk, tk, 2 * tq), F32)]


def _sample_phases(qrow_ref, qcol_ref, kn_ref, vn_ref, ga_ref, lq1, lk1, lq2, lk2, subg_ref,
                   expand_ref, k_pages, v_pages, o_ref, sc_ref, p_ref, pn_ref, acc_ref):
    pages = len(k_pages)
    n_maps = 2 * N_HEADS

    def scores(c):
        for h in range(N_HEADS):
            rows = slice(h * 2 * DH, (h + 1) * 2 * DH)
            q_h = qcol_ref[0, rows, :]
            for j, kp in enumerate(k_pages):
                prod = kp[rows, :] * q_h
                lanes = slice(j * CHUNK, (j + 1) * CHUNK)
                sc_ref[c, h:h + 1, lanes] = jnp.sum(prod[:DH, :], axis=0, keepdims=True)
                sc_ref[c, N_HEADS + h:N_HEADS + h + 1, lanes] = jnp.sum(
                    prod[DH:, :], axis=0, keepdims=True)

    def softmax():
        s = sc_ref[...]
        row = lax.broadcasted_iota(jnp.int32, (n_maps, D_ATTN), 0)
        seg = lax.broadcasted_iota(jnp.int32, (n_maps, D_ATTN), 1) // DH
        want = jnp.where(row < N_HEADS, 2 * row, 2 * (row - N_HEADS) + 1)
        qk_new = jnp.broadcast_to(qrow_ref[0] * kn_ref[0], (n_maps, D_ATTN))
        s_new = jnp.sum(jnp.where(seg == want, qk_new, 0.0), axis=-1, keepdims=True)
        m = jnp.max(jnp.max(s, axis=-1, keepdims=True), axis=0)
        m = jnp.maximum(m, s_new)
        p = jnp.exp2(s - m[None])
        pn = jnp.exp2(s_new - m)
        l = jnp.sum(jnp.sum(p, axis=-1, keepdims=True), axis=0) + pn
        inv_l = 1.0 / l
        p_ref[...] = (p * inv_l[None]).astype(BF16)
        pn_ref[...] = pn * inv_l
        acc_ref[...] = jnp.zeros(acc_ref.shape, F32)

    def values(c):
        pstack = jnp.concatenate(
            [p_ref[c, :, j * CHUNK:(j + 1) * CHUNK] for j in range(pages)], axis=0)
        pexp = jnp.dot(pstack, expand_ref[...], preferred_element_type=F32)
        row = lax.broadcasted_iota(jnp.int32, pexp.shape, 0)
        col = lax.broadcasted_iota(jnp.int32, pexp.shape, 1)
        pexp = jnp.where(row % N_HEADS == col % N_HEADS, pexp, 0.0).astype(BF16)
        acc = acc_ref[...]
        for j, vp in enumerate(v_pages):
            acc += jnp.dot(pexp[j * n_maps:(j + 1) * n_maps, :], vp[...].astype(BF16),
                           preferred_element_type=F32)
        acc_ref[...] = acc

    def finish():
        lam = _lam(lq1, lk1, lq2, lk2)
        v_new = vn_ref[0]
        full = acc_ref[...] + pn_ref[...] * jnp.concatenate([v_new, v_new], axis=0)
        o = full[:N_HEADS, :] - lam * full[N_HEADS:, :]
        ms = jnp.mean(o * o, axis=-1, keepdims=True)
        on = o * lax.rsqrt(ms + EPS) * subg_ref[...] * (1.0 - LAM_INIT)
        o_ref[0] = (on * _silu(ga_ref[0])).astype(o_ref.dtype)

    return scores, softmax, values, finish


def _sample_attn_kernel(pt_ref, qrow_ref, qcol_ref, kn_ref, vn_ref, ga_ref, lq1, lk1, lq2, lk2,
                        subg_ref, expand_ref, *rest, pages, n_chunks):
    del pt_ref
    scores, softmax, values, finish = _sample_phases(
        qrow_ref, qcol_ref, kn_ref, vn_ref, ga_ref, lq1, lk1, lq2, lk2, subg_ref, expand_ref,
        rest[:pages], rest[pages:2 * pages], *rest[2 * pages:])
    c = pl.program_id(1)
    pl.when(c < n_chunks)(lambda: scores(c))
    pl.when(c == n_chunks - 1)(softmax)
    pl.when(c >= n_chunks)(lambda: values(c - n_chunks))
    pl.when(c == 2 * n_chunks - 1)(finish)


def _fused_attn_kernel(pt_ref, q_ref, k_ref, v_ref, gap_ref, lq1, lk1, lq2, lk2, subg_ref,
                       qrow_ref, qcol_ref, kn_ref, vn_ref, gas_ref, expand_ref, *rest,
                       tq, tk, pages, n_chunks, n_seq):
    del pt_ref
    k_pages, v_pages = rest[:pages], rest[pages:2 * pages]
    op_ref, os_ref = rest[2 * pages:2 * pages + 2]
    n_prompt_scratch = len(_prompt_attn_scratch(k_ref.shape[0], tq, tk))
    prompt_scratch = rest[2 * pages + 2:2 * pages + 2 + n_prompt_scratch]
    sample_scratch = rest[2 * pages + 2 + n_prompt_scratch:]
    g = pl.program_id(0) * pl.num_programs(1) + pl.program_id(1)
    seq = g // n_chunks
    c = g % n_chunks
    scores, softmax, values, finish = _sample_phases(
        qrow_ref, qcol_ref, kn_ref, vn_ref, gas_ref, lq1, lk1, lq2, lk2, subg_ref, expand_ref,
        k_pages, v_pages, os_ref, *sample_scratch)
    last = c == n_chunks - 1
    trailing = (seq >= 1) & (seq <= n_seq)
    pl.when(trailing)(lambda: values(c))
    pl.when(trailing & last)(finish)
    pl.when(seq < n_seq)(lambda: scores(c))
    pl.when((seq < n_seq) & last)(softmax)

    _prompt_attn_step(pl.program_id(1), q_ref, k_ref, v_ref, gap_ref, lq1, lk1, lq2, lk2,
                      subg_ref, op_ref, *prompt_scratch, tq=tq, tk=tk)


def _sample_operands(page_table, q_s, k_s, v_s, ga_s, cache_k, cache_v):
    db = q_s.shape[0]
    n_pool = cache_k.shape[0]
    ck = jnp.transpose(cache_k, (0, 2, 3, 4, 1)).reshape(n_pool, D_ATTN, CHUNK)
    cv = cache_v.reshape(n_pool, CHUNK * N_HEADS, DV)
    expand = (jnp.arange(CHUNK * N_HEADS)[None, :] // N_HEADS
              == jnp.arange(CHUNK)[:, None]).astype(BF16)
    per_seq = (q_s.reshape(db, 1, D_ATTN),
               jnp.broadcast_to(q_s[:, :, None], (db, D_ATTN, LANES)),
               k_s.reshape(db, 1, D_ATTN),
               v_s.reshape(db, N_HEADS, DV),
               ga_s.reshape(db, N_HEADS, DV))
    return page_table.reshape(-1), per_seq, expand, ck, cv


_PER_SEQ_BLOCKS = ((1, D_ATTN), (D_ATTN, LANES), (1, D_ATTN), (N_HEADS, DV), (N_HEADS, DV))


def _sample_scratch(n_chunks, span):
    return [pltpu.VMEM((n_chunks, 2 * N_HEADS, span), F32),
            pltpu.VMEM((n_chunks, 2 * N_HEADS, span), BF16),
            pltpu.VMEM((2 * N_HEADS, 1), F32),
            pltpu.VMEM((2 * N_HEADS, DV), F32)]


def _sample_attn(operands, lq1, lk1, lq2, lk2, subg, *, first, count, n_pages, pages):
    pt, per_seq, expand, ck, cv = operands
    n_chunks = n_pages // pages

    def page(b, chunk, j, pt_):
        return pt_[(first + b) * n_pages + chunk * pages + j]

    per_b = lambda shp: pl.BlockSpec((1,) + shp, lambda b, c, pt_: (first + b, 0, 0))
    kspec = lambda j: pl.BlockSpec(
        (None, D_ATTN, CHUNK),
        lambda b, c, pt_: (page(b, jnp.minimum(c, n_chunks - 1), j, pt_), 0, 0))
    vspec = lambda j: pl.BlockSpec(
        (None, CHUNK * N_HEADS, DV),
        lambda b, c, pt_: (page(b, jnp.maximum(c - n_chunks, 0), j, pt_), 0, 0))
    small = lambda n: pl.BlockSpec((1, n), lambda b, c, pt_: (0, 0))
    grid_spec = pltpu.PrefetchScalarGridSpec(
        num_scalar_prefetch=1,
        grid=(count, 2 * n_chunks),
        in_specs=[per_b(shp) for shp in _PER_SEQ_BLOCKS]
                 + [small(DH), small(DH), small(DH), small(DH), small(DV),
                    pl.BlockSpec((CHUNK, CHUNK * N_HEADS), lambda b, c, pt_: (0, 0))]
                 + [kspec(j) for j in range(pages)] + [vspec(j) for j in range(pages)],
        out_specs=pl.BlockSpec((1, N_HEADS, DV), lambda b, c, pt_: (b, 0, 0)),
        scratch_shapes=_sample_scratch(n_chunks, pages * CHUNK),
    )
    return pl.pallas_call(
        functools.partial(_sample_attn_kernel, pages=pages, n_chunks=n_chunks),
        grid_spec=grid_spec,
        out_shape=jax.ShapeDtypeStruct((count, N_HEADS, DV), BF16),
        compiler_params=pltpu.CompilerParams(
            dimension_semantics=("parallel", "arbitrary"),
            vmem_limit_bytes=48 * MIB),
        name="sample_attn",
    )(pt, *per_seq, lq1, lk1, lq2, lk2, subg, expand, *([ck] * pages), *([cv] * pages))


def _fused_attn(z, kh, vh, operands, lq1, lk1, lq2, lk2, subg, *, tq, n_pages, pages):
    pt, per_seq, expand, ck, cv = operands
    s = z.shape[0]
    tk = tq // 2
    n_q = s // tq
    n_chunks = n_pages // pages
    db = per_seq[0].shape[0]
    n_seq = min(db, N_HEADS * n_q // n_chunks - 1)

    def step(h, i):
        return h * n_q + i

    def key_page(h, i, j, pt_):
        g = jnp.minimum(step(h, i), n_seq * n_chunks - 1)
        return pt_[(g // n_chunks) * n_pages + (g % n_chunks) * pages + j]

    def value_page(h, i, j, pt_):
        g = jnp.clip(step(h, i) - n_chunks, 0, n_seq * n_chunks - 1)
        return pt_[(g // n_chunks) * n_pages + (g % n_chunks) * pages + j]

    key_seq = lambda h, i: jnp.minimum(step(h, i) // n_chunks, n_seq - 1)
    value_seq = lambda h, i: jnp.clip(step(h, i) // n_chunks - 1, 0, n_seq - 1)
    seq_of = (key_seq, key_seq, key_seq, value_seq, value_seq)
    per_b = [pl.BlockSpec((1,) + shp, lambda h, i, pt_, f=f: (f(h, i), 0, 0))
             for shp, f in zip(_PER_SEQ_BLOCKS, seq_of)]
    kspec = lambda j: pl.BlockSpec((None, D_ATTN, CHUNK),
                                   lambda h, i, pt_: (key_page(h, i, j, pt_), 0, 0))
    vspec = lambda j: pl.BlockSpec((None, CHUNK * N_HEADS, DV),
                                   lambda h, i, pt_: (value_page(h, i, j, pt_), 0, 0))
    small = lambda n: pl.BlockSpec((1, n), lambda h, i, pt_: (0, 0))
    grid_spec = pltpu.PrefetchScalarGridSpec(
        num_scalar_prefetch=1,
        grid=(N_HEADS, n_q),
        in_specs=[pl.BlockSpec((tq, LANES), lambda h, i, pt_: (i, h)),
                  pl.BlockSpec((None, s, LANES), lambda h, i, pt_: (h, 0, 0)),
                  pl.BlockSpec((None, s, LANES), lambda h, i, pt_: (h, 0, 0)),
                  pl.BlockSpec((tq, LANES), lambda h, i, pt_: (i, Z_GATE_A * N_HEADS + h)),
                  small(DH), small(DH), small(DH), small(DH), small(DV)]
                 + per_b
                 + [pl.BlockSpec((CHUNK, CHUNK * N_HEADS), lambda h, i, pt_: (0, 0))]
                 + [kspec(j) for j in range(pages)] + [vspec(j) for j in range(pages)],
        out_specs=[pl.BlockSpec((tq, LANES), lambda h, i, pt_: (i, h)),
                   pl.BlockSpec((1, N_HEADS, DV), lambda h, i, pt_: (value_seq(h, i), 0, 0))],
        scratch_shapes=_prompt_attn_scratch(s, tq, tk) + _sample_scratch(n_chunks, pages * CHUNK),
    )
    a_p, a_s = pl.pallas_call(
        functools.partial(_fused_attn_kernel, tq=tq, tk=tk, pages=pages, n_chunks=n_chunks,
                          n_seq=n_seq),
        grid_spec=grid_spec,
        out_shape=[jax.ShapeDtypeStruct((s, D_ATTN), BF16),
                   jax.ShapeDtypeStruct((n_seq, N_HEADS, DV), BF16)],
        compiler_params=pltpu.CompilerParams(
            dimension_semantics=("arbitrary", "arbitrary"),
            vmem_limit_bytes=58 * MIB),
        name="fused_attn",
    )(pt, z, kh, vh, z, lq1, lk1, lq2, lk2, subg, *per_seq, expand,
      *([ck] * pages), *([cv] * pages))
    return a_p, a_s, n_seq


def _layernorm(x, g, b):
    mu = jnp.mean(x, axis=-1, keepdims=True)
    xc = x - mu
    var = jnp.mean(xc * xc, axis=-1, keepdims=True)
    return xc * lax.rsqrt(var + EPS) * g + b


def _sgate_prompt_kernel(u_ref, vsg_ref, gb_ref, lng_ref, lnb_ref, ws_ref, bias_ref, o_ref):
    rows = u_ref.shape[0]
    n_c = rows // CHUNK
    vn = _layernorm(vsg_ref[...].astype(F32), lng_ref[...], lnb_ref[...]).astype(BF16)
    t = lax.broadcasted_iota(jnp.int32, (CHUNK, CHUNK), 0)
    s = lax.broadcasted_iota(jnp.int32, (CHUNK, CHUNK), 1)
    causal = s <= t
    for g in range(N_GROUPS):
        cols = slice(g * SG_GROUP, (g + 1) * SG_GROUP)
        wm = jnp.where(causal, ws_ref[g], 0.0).astype(BF16)
        rhs = jnp.concatenate([vn[c * CHUNK:(c + 1) * CHUNK, cols] for c in range(n_c)], axis=1)
        sp = jnp.dot(wm, rhs, preferred_element_type=F32)
        for c in range(n_c):
            rws = slice(c * CHUNK, (c + 1) * CHUNK)
            sp_c = sp[:, c * SG_GROUP:(c + 1) * SG_GROUP] + bias_ref[:, cols]
            gb = gb_ref[rws, cols].astype(F32)
            o_ref[rws, cols] = (u_ref[rws, cols].astype(F32) * sp_c * _silu(gb)).astype(o_ref.dtype)


def _sgate_sample_kernel(u_ref, vsg_ref, gb_ref, lng_ref, lnb_ref, w0_ref, b0_ref, o_ref, vn_ref):
    vn = _layernorm(vsg_ref[...], lng_ref[...], lnb_ref[...])
    vn_ref[...] = vn
    sp = w0_ref[...] * vn + b0_ref[...]
    o_ref[...] = (u_ref[...] * sp * _silu(gb_ref[...])).astype(o_ref.dtype)


def _sgate_sample(zs, ln_g, ln_b, w0, b0):
    db = zs.shape[0]
    const2 = lambda shp: pl.BlockSpec(shp, lambda i: (0, 0))
    return pl.pallas_call(
        _sgate_sample_kernel,
        grid=(1,),
        in_specs=[
            pl.BlockSpec((db, D_SG), lambda i: (0, Z_U)),
            pl.BlockSpec((db, D_SG), lambda i: (0, Z_VSG)),
            pl.BlockSpec((db, D_SG), lambda i: (0, Z_GATE_B)),
            const2((1, D_SG)), const2((1, D_SG)), const2((1, D_SG)), const2((1, D_SG)),
        ],
        out_specs=[pl.BlockSpec((db, D_SG), lambda i: (0, 0)),
                   pl.BlockSpec((db, D_SG), lambda i: (0, 0))],
        out_shape=[jax.ShapeDtypeStruct((db, D_SG), BF16),
                   jax.ShapeDtypeStruct((db, D_SG), F32)],
        name="sgate_sample",
    )(zs, zs, zs, ln_g, ln_b, w0, b0)


def _merge_kernel(x_ref, a_ref, b_ref, ga0_ref, ga1_ref, gb0_ref, gb1_ref,
                  wpa_ref, wpb_ref, wo_ref, np_ref, o_ref):
    ya = jnp.dot(a_ref[...], wpa_ref[...], preferred_element_type=F32)
    yb = jnp.dot(b_ref[...], wpb_ref[...], preferred_element_type=F32)
    half = D_MODEL // 2
    sig = lambda r: jax.nn.sigmoid(r[...].astype(F32))
    m_lo = sig(ga0_ref) * ya[:, :half] + sig(gb0_ref) * yb[:, :half]
    m_hi = sig(ga1_ref) * ya[:, half:] + sig(gb1_ref) * yb[:, half:]
    m = jnp.concatenate([m_lo, m_hi], axis=1).astype(BF16)
    mo = jnp.dot(m, wo_ref[...], preferred_element_type=F32)
    ms = jnp.mean(mo * mo, axis=-1, keepdims=True)
    o_ref[...] = x_ref[...] + mo * lax.rsqrt(ms + EPS) * np_ref[...]


def _merge(x, a_in, b_in, z, wpa_bf, wpb_bf, wo_bf, norm_post, *, tm):
    m = x.shape[0]
    resident = lambda shp: pl.BlockSpec(shp, lambda i: (0, 0), pipeline_mode=pl.Buffered(1))
    zcol = lambda jcol: pl.BlockSpec((tm, COL_BLOCK), lambda i: (i, jcol))
    return pl.pallas_call(
        _merge_kernel,
        grid=(m // tm,),
        in_specs=[
            pl.BlockSpec((tm, D_MODEL), lambda i: (i, 0)),
            pl.BlockSpec((tm, D_ATTN), lambda i: (i, 0)),
            pl.BlockSpec((tm, D_SG), lambda i: (i, 0)),
            zcol(Z_GA), zcol(Z_GA + 1), zcol(Z_GB), zcol(Z_GB + 1),
            resident((D_ATTN, D_MODEL)), resident((D_SG, D_MODEL)), resident((D_MODEL, D_MODEL)),
            resident((1, D_MODEL)),
        ],
        out_specs=pl.BlockSpec((tm, D_MODEL), lambda i: (i, 0)),
        out_shape=jax.ShapeDtypeStruct((m, D_MODEL), F32),
        compiler_params=pltpu.CompilerParams(dimension_semantics=("parallel",),
                                             vmem_limit_bytes=56 * MIB),
        name="merge",
    )(x, a_in, b_in, z, z, z, z, wpa_bf, wpb_bf, wo_bf, norm_post)


def _sgate_merge_kernel(x_ref, a_ref, u_ref, vsg_ref, gbt_ref, lng_ref, lnb_ref, ws_ref, bias_ref,
                        ga0_ref, ga1_ref, gb0_ref, gb1_ref, wpa_ref, wpb_ref, wo_ref, np_ref,
                        o_ref, b_ref):
    _sgate_prompt_kernel(u_ref, vsg_ref, gbt_ref, lng_ref, lnb_ref, ws_ref, bias_ref, b_ref)
    _merge_kernel(x_ref, a_ref, b_ref, ga0_ref, ga1_ref, gb0_ref, gb1_ref,
                  wpa_ref, wpb_ref, wo_ref, np_ref, o_ref)


def _sgate_merge(x, a_in, z, ln_g, ln_b, w_s, bias, wpa_bf, wpb_bf, wo_bf, norm_post, *, tm):
    m = x.shape[0]
    resident = lambda shp: pl.BlockSpec(shp, lambda i: (0,) * len(shp),
                                        pipeline_mode=pl.Buffered(1))
    zcol = lambda jcol: pl.BlockSpec((tm, COL_BLOCK), lambda i: (i, jcol))
    return pl.pallas_call(
        _sgate_merge_kernel,
        grid=(m // tm,),
        in_specs=[
            pl.BlockSpec((tm, D_MODEL), lambda i: (i, 0)),
            pl.BlockSpec((tm, D_ATTN), lambda i: (i, 0)),
            zcol(Z_U), zcol(Z_VSG), zcol(Z_GATE_B),
            resident((1, D_SG)), resident((1, D_SG)),
            resident((N_GROUPS, CHUNK, CHUNK)), resident((CHUNK, D_SG)),
            zcol(Z_GA), zcol(Z_GA + 1), zcol(Z_GB), zcol(Z_GB + 1),
            resident((D_ATTN, D_MODEL)), resident((D_SG, D_MODEL)), resident((D_MODEL, D_MODEL)),
            resident((1, D_MODEL)),
        ],
        out_specs=pl.BlockSpec((tm, D_MODEL), lambda i: (i, 0)),
        out_shape=jax.ShapeDtypeStruct((m, D_MODEL), F32),
        scratch_shapes=[pltpu.VMEM((tm, D_SG), BF16)],
        compiler_params=pltpu.CompilerParams(dimension_semantics=("parallel",),
                                             vmem_limit_bytes=56 * MIB),
        name="sgate_merge",
    )(x, a_in, z, z, z, ln_g, ln_b, w_s, bias, z, z, z, z, wpa_bf, wpb_bf, wo_bf, norm_post)


def _rope_tables(pos):
    half = DH // 2
    inv = ROPE_THETA ** (-(jnp.arange(half, dtype=F32) * 2.0 / DH))
    ang = pos.astype(F32)[:, None] * inv[None, :]
    cos = jnp.cos(ang)
    sin = jnp.sin(ang)
    reps = LANES // DH
    cos_t = jnp.tile(jnp.concatenate([cos, cos], axis=1), (1, reps))
    sin_t = jnp.tile(jnp.concatenate([-sin, sin], axis=1), (1, reps))
    return cos_t, sin_t


def kernel(x_prompt, x_sample, cache_k, cache_v, page_table, norm_pre, w_in, lam_q1, lam_k1,
           lam_q2, lam_k2, sub_g, w_pa, ln_g, ln_b, w_s, b_s, w_pb, w_o, norm_post):
    b, s, _ = x_prompt.shape
    db, t, _ = x_sample.shape
    assert b == 1 and t == 1 and norm_pre.shape[0] == 1
    n_past = page_table.shape[1] * cache_k.shape[2]

    wpa_bf = w_pa[0].astype(BF16)
    wpb_bf = w_pb[0].astype(BF16)
    wo_bf = w_o[0].astype(BF16)
    lams = (lam_q1, lam_k1, lam_q2, lam_k2)

    xs = x_sample.reshape(db, D_MODEL)
    cos_s, sin_s = _rope_tables(jnp.full((db,), n_past, dtype=jnp.int32))
    z_s, k_s, v_s, _, _, w_in_bf = _inproj(xs, norm_pre, w_in[0], cos_s, sin_s, tm=db,
                                           z_dtype=F32)
    sample_ops = _sample_operands(page_table, z_s[:, :D_ATTN], k_s, v_s,
                                  z_s[:, Z_GATE_A * COL_BLOCK:(Z_GATE_A + 1) * COL_BLOCK], cache_k[0], cache_v[0])
    n_pages = page_table.shape[1]

    xp = x_prompt.reshape(s, D_MODEL)
    cos_p, sin_p = _rope_tables(jnp.arange(s, dtype=jnp.int32))
    z_p, k_p, v_p, kh_p, vh_p = _inproj(xp, norm_pre, w_in_bf, cos_p, sin_p, tm=1024,
                                        z_dtype=BF16)
    a_p, a_s, n_seq = _fused_attn(z_p, kh_p, vh_p, sample_ops, *lams, sub_g, tq=512,
                                  n_pages=n_pages, pages=SAMPLE_PAGES)
    bias_p = jnp.repeat(jnp.transpose(b_s[0]), SG_GROUP, axis=1)
    y_p = _sgate_merge(xp, a_p, z_p, ln_g, ln_b, w_s[0], bias_p, wpa_bf, wpb_bf, wo_bf,
                       norm_post, tm=256)

    if n_seq < db:
        a_rest = _sample_attn(sample_ops, *lams, sub_g, first=n_seq, count=db - n_seq,
                              n_pages=n_pages, pages=SAMPLE_PAGES)
        a_s = jnp.concatenate([a_s, a_rest], axis=0)
    a_s = a_s.reshape(db, D_ATTN)
    w0 = jnp.repeat(w_s[0, :, 0, 0], SG_GROUP)[None, :]
    b0 = jnp.repeat(b_s[0, :, 0], SG_GROUP)[None, :]
    b_sm, vn_s = _sgate_sample(z_s, ln_g, ln_b, w0, b0)
    y_s = _merge(xs, a_s, b_sm, z_s, wpa_bf, wpb_bf, wo_bf, norm_post, tm=db)

    return (y_p.reshape(b, s, D_MODEL),
            y_s.reshape(db, t, D_MODEL),
            k_p.reshape(1, b, s, N_HEADS, 2, DH),
            v_p.reshape(1, b, s, N_HEADS, DV),
            k_s.reshape(1, db, t, N_HEADS, 2, DH),
            v_s.reshape(1, db, t, N_HEADS, DV),
            vn_s.reshape(1, db, t, D_SG))
```

```python
import functools
import math

import jax
import jax.numpy as jnp
from jax import lax
from jax.experimental import pallas as pl
from jax.experimental.pallas import tpu as pltpu

F32 = jnp.float32
BF16 = jnp.bfloat16

D_MODEL = 2048
D_ATTN = D_MODEL // 2
N_HEADS = 8
DH = D_ATTN // (2 * N_HEADS)
DV = 2 * DH
D_SG = D_MODEL // 2
N_GROUPS = 8
SG_GROUP = D_SG // N_GROUPS
CHUNK = 128
ROPE_THETA = 10000.0
EPS = 1e-6
NEG = -1e30
D_IN = 4 * D_ATTN + 3 * D_SG + 2 * D_MODEL
COL_BLOCK = 1024
N_COL_BLOCKS = D_IN // COL_BLOCK
Z_Q, Z_GATE_A, Z_U, Z_VSG, Z_GATE_B, Z_GA, Z_GB = 0, 1, 2, 3, 4, 5, 7
Z_WIDTH = D_IN - 2 * D_ATTN
LANES = 128
Q_SCALE = DH ** -0.5 * math.log2(math.e)
LAM_INIT = 0.8 - 0.6 * math.exp(-0.3 * 0)
MIB = 1024 * 1024
ONES_ROWS = 16
SAMPLE_PAGES = 16


def _silu(x):
    return x * jax.nn.sigmoid(x)


def _lam(lq1, lk1, lq2, lk2):
    a = jnp.sum(lq1[...] * lk1[...], axis=-1, keepdims=True)
    b = jnp.sum(lq2[...] * lk2[...], axis=-1, keepdims=True)
    return jnp.exp(a) - jnp.exp(b) + LAM_INIT


def _inproj_kernel(x_ref, g_ref, w_ref, cos_ref, sin_ref, z_ref, k_ref, v_ref, kh_ref, vh_ref,
                   *rest):
    xn_ref = rest[-1]

    def by_head(dst, a):
        for h in range(N_HEADS):
            dst[h] = a[:, h * LANES:(h + 1) * LANES].astype(dst.dtype)

    j = pl.program_id(1)
    tm = x_ref.shape[0]

    @pl.when(j == 0)
    def _():
        x = x_ref[...]
        ms = jnp.mean(x * x, axis=-1, keepdims=True)
        xn_ref[...] = (x * lax.rsqrt(ms + EPS) * g_ref[...]).astype(BF16)

    w = w_ref[...].astype(BF16)
    if len(rest) == 2:
        rest[0][...] = w
    acc = jnp.dot(xn_ref[...], w, preferred_element_type=F32)

    def rope(a):
        cos = cos_ref[...]
        sin = sin_ref[...]
        lane = lax.broadcasted_iota(jnp.int32, (tm, LANES), 1)
        first_half = (lane % DH) < (DH // 2)
        outs = []
        for c in range(COL_BLOCK // LANES):
            blk = a[:, c * LANES:(c + 1) * LANES]
            partner = jnp.where(first_half,
                                pltpu.roll(blk, LANES - DH // 2, 1),
                                pltpu.roll(blk, DH // 2, 1))
            outs.append(blk * cos + partner * sin)
        return jnp.concatenate(outs, axis=1)

    @pl.when(j == 0)
    def _():
        z_ref[...] = (rope(acc) * Q_SCALE).astype(z_ref.dtype)

    @pl.when(j == 1)
    def _():
        r = rope(acc)
        k_ref[...] = r
        by_head(kh_ref, r)

    @pl.when(j == 2)
    def _():
        v_ref[...] = acc
        by_head(vh_ref, acc)

    @pl.when(j > 2)
    def _():
        z_ref[...] = acc.astype(z_ref.dtype)


def _inproj(x, g, w_bf, cos_t, sin_t, *, tm, z_dtype):
    m = x.shape[0]
    emit_w = w_bf.dtype != BF16
    assert not emit_w or m == tm
    w_out_spec = [pl.BlockSpec((D_MODEL, COL_BLOCK), lambda i, j: (0, j))] if emit_w else []
    w_out_shape = [jax.ShapeDtypeStruct((D_MODEL, D_IN), BF16)] if emit_w else []
    return pl.pallas_call(
        _inproj_kernel,
        grid=(m // tm, N_COL_BLOCKS),
        in_specs=[
            pl.BlockSpec((tm, D_MODEL), lambda i, j: (i, 0)),
            pl.BlockSpec((1, D_MODEL), lambda i, j: (0, 0)),
            pl.BlockSpec((D_MODEL, COL_BLOCK), lambda i, j: (0, j)),
            pl.BlockSpec((tm, LANES), lambda i, j: (i, 0)),
            pl.BlockSpec((tm, LANES), lambda i, j: (i, 0)),
        ],
        out_specs=[
            pl.BlockSpec((tm, COL_BLOCK), lambda i, j: (i, jnp.maximum(j - 2, 0))),
            pl.BlockSpec((tm, COL_BLOCK), lambda i, j: (i, 0), pipeline_mode=pl.Buffered(1)),
            pl.BlockSpec((tm, COL_BLOCK), lambda i, j: (i, 0), pipeline_mode=pl.Buffered(1)),
            pl.BlockSpec((N_HEADS, tm, LANES), lambda i, j: (0, i, 0), pipeline_mode=pl.Buffered(1)),
            pl.BlockSpec((N_HEADS, tm, LANES), lambda i, j: (0, i, 0), pipeline_mode=pl.Buffered(1)),
        ] + w_out_spec,
        out_shape=[
            jax.ShapeDtypeStruct((m, Z_WIDTH), z_dtype),
            jax.ShapeDtypeStruct((m, D_ATTN), F32),
            jax.ShapeDtypeStruct((m, D_ATTN), F32),
            jax.ShapeDtypeStruct((N_HEADS, m, LANES), BF16),
            jax.ShapeDtypeStruct((N_HEADS, m, LANES), BF16),
        ] + w_out_shape,
        scratch_shapes=[pltpu.VMEM((tm, D_MODEL), BF16)],
        compiler_params=pltpu.CompilerParams(
            dimension_semantics=("parallel", "arbitrary"),
            vmem_limit_bytes=60 * MIB),
        name="inproj",
    )(x, g, w_bf, cos_t, sin_t)


def _prompt_attn_step(qi, q_ref, k_ref, v_ref, ga_ref, lq1, lk1, lq2, lk2, subg_ref, o_ref,
                      vt_ref, qq_ref, sa_ref, sb_ref, m_ref, acc_ref, *, tq, tk):
    n_kv = k_ref.shape[0] // tk

    @pl.when(qi == 0)
    def _():
        def transpose_block(kb, carry):
            start = pl.multiple_of(kb * tk, tk)
            vt_ref[kb, :DV, :] = v_ref[pl.ds(start, tk), :].astype(F32).T.astype(BF16)
            vt_ref[kb, DV:, :] = jnp.ones((ONES_ROWS, tk), BF16)
            return carry
        lax.fori_loop(0, n_kv, transpose_block, 0)

    q_t = q_ref[...].astype(F32).T
    feat = lax.broadcasted_iota(jnp.int32, (LANES, tq), 0)
    qq_ref[:, :tq] = jnp.where(feat < DH, q_t, 0.0).astype(BF16)
    qq_ref[:, tq:] = jnp.where(feat >= DH, q_t, 0.0).astype(BF16)

    m_ref[...] = jnp.full(m_ref.shape, NEG, F32)
    acc_ref[...] = jnp.zeros(acc_ref.shape, F32)

    def scores(kb, dst):
        start = pl.multiple_of(kb * tk, tk)
        dst[...] = jnp.dot(k_ref[pl.ds(start, tk), :], qq_ref[...],
                           preferred_element_type=F32)

    def consume(src, kb, masked):
        s_t = src[...]
        if masked:
            key = kb * tk + lax.broadcasted_iota(jnp.int32, (tk, 2 * tq), 0)
            qry = lax.broadcasted_iota(jnp.int32, (tk, 2 * tq), 1)
            qry = qi * tq + jnp.where(qry >= tq, qry - tq, qry)
            s_t = jnp.where(key <= qry, s_t, NEG)
        m_old = m_ref[...]
        m_new = jnp.maximum(m_old, jnp.max(s_t, axis=0, keepdims=True))
        alpha = jnp.exp2(m_old - m_new)
        p_t = jnp.exp2(s_t - m_new)
        acc_ref[...] = alpha * acc_ref[...] + jnp.dot(vt_ref[kb], p_t.astype(BF16),
                                                      preferred_element_type=F32)
        m_ref[...] = m_new

    assert tq == 2 * tk
    scores(0, sa_ref)

    def pair(first):
        scores(first + 1, sb_ref)
        consume(sa_ref, first, False)
        scores(first + 2, sa_ref)
        consume(sb_ref, first + 1, False)

    def four_pairs(p, carry):
        for r in range(4):
            pair(8 * p + 2 * r)
        return carry

    lax.fori_loop(0, qi // 4, four_pairs, 0)
    left = 8 * (qi // 4)

    @pl.when(qi % 4 >= 2)
    def _():
        pair(left)
        pair(left + 2)

    @pl.when(qi % 2 == 1)
    def _():
        pair(2 * qi - 2)

    scores(2 * qi + 1, sb_ref)
    consume(sa_ref, 2 * qi, True)
    consume(sb_ref, 2 * qi + 1, True)

    lam = _lam(lq1, lk1, lq2, lk2)
    o_all = acc_ref[:DV, :] / acc_ref[DV:DV + 1, :]
    o = (o_all[:, :tq] - lam * o_all[:, tq:]).T
    ms = jnp.mean(o * o, axis=-1, keepdims=True)
    on = o * lax.rsqrt(ms + EPS) * subg_ref[...] * (1.0 - LAM_INIT)
    ga = ga_ref[...].astype(F32)
    o_ref[...] = (on * _silu(ga)).astype(o_ref.dtype)


def _prompt_attn_scratch(s, tq, tk):
    return [pltpu.VMEM((s // tk, DV + ONES_ROWS, tk), BF16),
            pltpu.VMEM((LANES, 2 * tq), BF16),
            pltpu.VMEM((tk, 2 * tq), F32), pltpu.VMEM((tk, 2 * tq), F32),
            pltpu.VMEM((1, 2 * tq), F32),
            pltpu.VMEM((DV + ONES_ROWS, 2 * tq), F32)]


def _sample_phases(qrow_ref, qcol_ref, kn_ref, vn_ref, ga_ref, lq1, lk1, lq2, lk2, subg_ref,
                   expand_ref, k_pages, v_pages, o_ref, sc_ref, p_ref, pn_ref, acc_ref):
    pages = len(k_pages)
    n_maps = 2 * N_HEADS

    def scores(c):
        for h in range(N_HEADS):
            rows = slice(h * 2 * DH, (h + 1) * 2 * DH)
            q_h = qcol_ref[0, rows, :]
            for j, kp in enumerate(k_pages):
                prod = kp[rows, :] * q_h
                lanes = slice(j * CHUNK, (j + 1) * CHUNK)
                sc_ref[c, h:h + 1, lanes] = jnp.sum(prod[:DH, :], axis=0, keepdims=True)
                sc_ref[c, N_HEADS + h:N_HEADS + h + 1, lanes] = jnp.sum(
                    prod[DH:, :], axis=0, keepdims=True)

    def softmax():
        s = sc_ref[...]
        row = lax.broadcasted_iota(jnp.int32, (n_maps, D_ATTN), 0)
        seg = lax.broadcasted_iota(jnp.int32, (n_maps, D_ATTN), 1) // DH
        want = jnp.where(row < N_HEADS, 2 * row, 2 * (row - N_HEADS) + 1)
        qk_new = jnp.broadcast_to(qrow_ref[0] * kn_ref[0], (n_maps, D_ATTN))
        s_new = jnp.sum(jnp.where(seg == want, qk_new, 0.0), axis=-1, keepdims=True)
        m = jnp.max(jnp.max(s, axis=-1, keepdims=True), axis=0)
        m = jnp.maximum(m, s_new)
        p = jnp.exp2(s - m[None])
        pn = jnp.exp2(s_new - m)
        l = jnp.sum(jnp.sum(p, axis=-1, keepdims=True), axis=0) + pn
        inv_l = 1.0 / l
        p_ref[...] = (p * inv_l[None]).astype(BF16)
        pn_ref[...] = pn * inv_l
        acc_ref[...] = jnp.zeros(acc_ref.shape, F32)

    def values(c):
        pstack = jnp.concatenate(
            [p_ref[c, :, j * CHUNK:(j + 1) * CHUNK] for j in range(pages)], axis=0)
        pexp = jnp.dot(pstack, expand_ref[...], preferred_element_type=F32)
        row = lax.broadcasted_iota(jnp.int32, pexp.shape, 0)
        col = lax.broadcasted_iota(jnp.int32, pexp.shape, 1)
        pexp = jnp.where(row % N_HEADS == col % N_HEADS, pexp, 0.0).astype(BF16)
        acc = acc_ref[...]
        for j, vp in enumerate(v_pages):
            acc += jnp.dot(pexp[j * n_maps:(j + 1) * n_maps, :], vp[...].astype(BF16),
                           preferred_element_type=F32)
        acc_ref[...] = acc

    def finish():
        lam = _lam(lq1, lk1, lq2, lk2)
        v_new = vn_ref[0]
        full = acc_ref[...] + pn_ref[...] * jnp.concatenate([v_new, v_new], axis=0)
        o = full[:N_HEADS, :] - lam * full[N_HEADS:, :]
        ms = jnp.mean(o * o, axis=-1, keepdims=True)
        on = o * lax.rsqrt(ms + EPS) * subg_ref[...] * (1.0 - LAM_INIT)
        o_ref[0] = (on * _silu(ga_ref[0])).astype(o_ref.dtype)

    return scores, softmax, values, finish


def _sample_attn_kernel(pt_ref, qrow_ref, qcol_ref, kn_ref, vn_ref, ga_ref, lq1, lk1, lq2, lk2,
                        subg_ref, expand_ref, *rest, pages, n_chunks):
    del pt_ref
    scores, softmax, values, finish = _sample_phases(
        qrow_ref, qcol_ref, kn_ref, vn_ref, ga_ref, lq1, lk1, lq2, lk2, subg_ref, expand_ref,
        rest[:pages], rest[pages:2 * pages], *rest[2 * pages:])
    c = pl.program_id(1)
    pl.when(c < n_chunks)(lambda: scores(c))
    pl.when(c == n_chunks - 1)(softmax)
    pl.when(c >= n_chunks)(lambda: values(c - n_chunks))
    pl.when(c == 2 * n_chunks - 1)(finish)


def _fused_attn_kernel(pt_ref, q_ref, k_ref, v_ref, gap_ref, lq1, lk1, lq2, lk2, subg_ref,
                       qrow_ref, qcol_ref, kn_ref, vn_ref, gas_ref, expand_ref, *rest,
                       tq, tk, pages, n_chunks, n_seq):
    del pt_ref
    k_pages, v_pages = rest[:pages], rest[pages:2 * pages]
    op_ref, os_ref = rest[2 * pages:2 * pages + 2]
    prompt_scratch = rest[2 * pages + 2:2 * pages + 8]
    sample_scratch = rest[2 * pages + 8:]
    g = pl.program_id(0) * pl.num_programs(1) + pl.program_id(1)
    seq = g // n_chunks
    c = g % n_chunks
    scores, softmax, values, finish = _sample_phases(
        qrow_ref, qcol_ref, kn_ref, vn_ref, gas_ref, lq1, lk1, lq2, lk2, subg_ref, expand_ref,
        k_pages, v_pages, os_ref, *sample_scratch)
    last = c == n_chunks - 1
    trailing = (seq >= 1) & (seq <= n_seq)
    pl.when(trailing)(lambda: values(c))
    pl.when(trailing & last)(finish)
    pl.when(seq < n_seq)(lambda: scores(c))
    pl.when((seq < n_seq) & last)(softmax)

    _prompt_attn_step(pl.program_id(1), q_ref, k_ref, v_ref, gap_ref, lq1, lk1, lq2, lk2,
                      subg_ref, op_ref, *prompt_scratch, tq=tq, tk=tk)


def _sample_operands(page_table, q_s, k_s, v_s, ga_s, cache_k, cache_v):
    db = q_s.shape[0]
    n_pool = cache_k.shape[0]
    ck = jnp.transpose(cache_k, (0, 2, 3, 4, 1)).reshape(n_pool, D_ATTN, CHUNK)
    cv = cache_v.reshape(n_pool, CHUNK * N_HEADS, DV)
    expand = (jnp.arange(CHUNK * N_HEADS)[None, :] // N_HEADS
              == jnp.arange(CHUNK)[:, None]).astype(BF16)
    per_seq = (q_s.reshape(db, 1, D_ATTN),
               jnp.broadcast_to(q_s[:, :, None], (db, D_ATTN, LANES)),
               k_s.reshape(db, 1, D_ATTN),
               v_s.reshape(db, N_HEADS, DV),
               ga_s.reshape(db, N_HEADS, DV))
    return page_table.reshape(-1), per_seq, expand, ck, cv


_PER_SEQ_BLOCKS = ((1, D_ATTN), (D_ATTN, LANES), (1, D_ATTN), (N_HEADS, DV), (N_HEADS, DV))


def _sample_scratch(n_chunks, span):
    return [pltpu.VMEM((n_chunks, 2 * N_HEADS, span), F32),
            pltpu.VMEM((n_chunks, 2 * N_HEADS, span), BF16),
            pltpu.VMEM((2 * N_HEADS, 1), F32),
            pltpu.VMEM((2 * N_HEADS, DV), F32)]


def _sample_attn(operands, lq1, lk1, lq2, lk2, subg, *, first, count, n_pages, pages):
    pt, per_seq, expand, ck, cv = operands
    n_chunks = n_pages // pages

    def page(b, chunk, j, pt_):
        return pt_[(first + b) * n_pages + chunk * pages + j]

    per_b = lambda shp: pl.BlockSpec((1,) + shp, lambda b, c, pt_: (first + b, 0, 0))
    kspec = lambda j: pl.BlockSpec(
        (None, D_ATTN, CHUNK),
        lambda b, c, pt_: (page(b, jnp.minimum(c, n_chunks - 1), j, pt_), 0, 0))
    vspec = lambda j: pl.BlockSpec(
        (None, CHUNK * N_HEADS, DV),
        lambda b, c, pt_: (page(b, jnp.maximum(c - n_chunks, 0), j, pt_), 0, 0))
    small = lambda n: pl.BlockSpec((1, n), lambda b, c, pt_: (0, 0))
    grid_spec = pltpu.PrefetchScalarGridSpec(
        num_scalar_prefetch=1,
        grid=(count, 2 * n_chunks),
        in_specs=[per_b(shp) for shp in _PER_SEQ_BLOCKS]
                 + [small(DH), small(DH), small(DH), small(DH), small(DV),
                    pl.BlockSpec((CHUNK, CHUNK * N_HEADS), lambda b, c, pt_: (0, 0))]
                 + [kspec(j) for j in range(pages)] + [vspec(j) for j in range(pages)],
        out_specs=pl.BlockSpec((1, N_HEADS, DV), lambda b, c, pt_: (b, 0, 0)),
        scratch_shapes=_sample_scratch(n_chunks, pages * CHUNK),
    )
    return pl.pallas_call(
        functools.partial(_sample_attn_kernel, pages=pages, n_chunks=n_chunks),
        grid_spec=grid_spec,
        out_shape=jax.ShapeDtypeStruct((count, N_HEADS, DV), BF16),
        compiler_params=pltpu.CompilerParams(
            dimension_semantics=("parallel", "arbitrary"),
            vmem_limit_bytes=48 * MIB),
        name="sample_attn",
    )(pt, *per_seq, lq1, lk1, lq2, lk2, subg, expand, *([ck] * pages), *([cv] * pages))


def _fused_attn(z, kh, vh, operands, lq1, lk1, lq2, lk2, subg, *, tq, n_pages, pages):
    pt, per_seq, expand, ck, cv = operands
    s = z.shape[0]
    tk = tq // 2
    n_q = s // tq
    n_chunks = n_pages // pages
    db = per_seq[0].shape[0]
    n_seq = min(db, N_HEADS * n_q // n_chunks - 1)

    def step(h, i):
        return h * n_q + i

    def key_page(h, i, j, pt_):
        g = jnp.minimum(step(h, i), n_seq * n_chunks - 1)
        return pt_[(g // n_chunks) * n_pages + (g % n_chunks) * pages + j]

    def value_page(h, i, j, pt_):
        g = jnp.clip(step(h, i) - n_chunks, 0, n_seq * n_chunks - 1)
        return pt_[(g // n_chunks) * n_pages + (g % n_chunks) * pages + j]

    key_seq = lambda h, i: jnp.minimum(step(h, i) // n_chunks, n_seq - 1)
    value_seq = lambda h, i: jnp.clip(step(h, i) // n_chunks - 1, 0, n_seq - 1)
    seq_of = (key_seq, key_seq, key_seq, value_seq, value_seq)
    per_b = [pl.BlockSpec((1,) + shp, lambda h, i, pt_, f=f: (f(h, i), 0, 0))
             for shp, f in zip(_PER_SEQ_BLOCKS, seq_of)]
    kspec = lambda j: pl.BlockSpec((None, D_ATTN, CHUNK),
                                   lambda h, i, pt_: (key_page(h, i, j, pt_), 0, 0))
    vspec = lambda j: pl.BlockSpec((None, CHUNK * N_HEADS, DV),
                                   lambda h, i, pt_: (value_page(h, i, j, pt_), 0, 0))
    small = lambda n: pl.BlockSpec((1, n), lambda h, i, pt_: (0, 0))
    grid_spec = pltpu.PrefetchScalarGridSpec(
        num_scalar_prefetch=1,
        grid=(N_HEADS, n_q),
        in_specs=[pl.BlockSpec((tq, LANES), lambda h, i, pt_: (i, h)),
                  pl.BlockSpec((None, s, LANES), lambda h, i, pt_: (h, 0, 0)),
                  pl.BlockSpec((None, s, LANES), lambda h, i, pt_: (h, 0, 0)),
                  pl.BlockSpec((tq, LANES), lambda h, i, pt_: (i, Z_GATE_A * N_HEADS + h)),
                  small(DH), small(DH), small(DH), small(DH), small(DV)]
                 + per_b
                 + [pl.BlockSpec((CHUNK, CHUNK * N_HEADS), lambda h, i, pt_: (0, 0))]
                 + [kspec(j) for j in range(pages)] + [vspec(j) for j in range(pages)],
        out_specs=[pl.BlockSpec((tq, LANES), lambda h, i, pt_: (i, h)),
                   pl.BlockSpec((1, N_HEADS, DV), lambda h, i, pt_: (value_seq(h, i), 0, 0))],
        scratch_shapes=_prompt_attn_scratch(s, tq, tk) + _sample_scratch(n_chunks, pages * CHUNK),
    )
    a_p, a_s = pl.pallas_call(
        functools.partial(_fused_attn_kernel, tq=tq, tk=tk, pages=pages, n_chunks=n_chunks,
                          n_seq=n_seq),
        grid_spec=grid_spec,
        out_shape=[jax.ShapeDtypeStruct((s, D_ATTN), BF16),
                   jax.ShapeDtypeStruct((n_seq, N_HEADS, DV), BF16)],
        compiler_params=pltpu.CompilerParams(
            dimension_semantics=("arbitrary", "arbitrary"),
            vmem_limit_bytes=58 * MIB),
        name="fused_attn",
    )(pt, z, kh, vh, z, lq1, lk1, lq2, lk2, subg, *per_seq, expand,
      *([ck] * pages), *([cv] * pages))
    return a_p, a_s, n_seq


def _layernorm(x, g, b):
    mu = jnp.mean(x, axis=-1, keepdims=True)
    xc = x - mu
    var = jnp.mean(xc * xc, axis=-1, keepdims=True)
    return xc * lax.rsqrt(var + EPS) * g + b


def _sgate_prompt_kernel(u_ref, vsg_ref, gb_ref, lng_ref, lnb_ref, ws_ref, bias_ref, o_ref):
    rows = u_ref.shape[0]
    n_c = rows // CHUNK
    vn = _layernorm(vsg_ref[...].astype(F32), lng_ref[...], lnb_ref[...]).astype(BF16)
    t = lax.broadcasted_iota(jnp.int32, (CHUNK, CHUNK), 0)
    s = lax.broadcasted_iota(jnp.int32, (CHUNK, CHUNK), 1)
    causal = s <= t
    for g in range(N_GROUPS):
        cols = slice(g * SG_GROUP, (g + 1) * SG_GROUP)
        wm = jnp.where(causal, ws_ref[g], 0.0).astype(BF16)
        rhs = jnp.concatenate([vn[c * CHUNK:(c + 1) * CHUNK, cols] for c in range(n_c)], axis=1)
        sp = jnp.dot(wm, rhs, preferred_element_type=F32)
        for c in range(n_c):
            rws = slice(c * CHUNK, (c + 1) * CHUNK)
            sp_c = sp[:, c * SG_GROUP:(c + 1) * SG_GROUP] + bias_ref[:, cols]
            gb = gb_ref[rws, cols].astype(F32)
            o_ref[rws, cols] = (u_ref[rws, cols].astype(F32) * sp_c * _silu(gb)).astype(o_ref.dtype)


def _sgate_sample_kernel(u_ref, vsg_ref, gb_ref, lng_ref, lnb_ref, w0_ref, b0_ref, o_ref, vn_ref):
    vn = _layernorm(vsg_ref[...], lng_ref[...], lnb_ref[...])
    vn_ref[...] = vn
    sp = w0_ref[...] * vn + b0_ref[...]
    o_ref[...] = (u_ref[...] * sp * _silu(gb_ref[...])).astype(o_ref.dtype)


def _sgate_sample(zs, ln_g, ln_b, w0, b0):
    db = zs.shape[0]
    const2 = lambda shp: pl.BlockSpec(shp, lambda i: (0, 0))
    return pl.pallas_call(
        _sgate_sample_kernel,
        grid=(1,),
        in_specs=[
            pl.BlockSpec((db, D_SG), lambda i: (0, Z_U)),
            pl.BlockSpec((db, D_SG), lambda i: (0, Z_VSG)),
            pl.BlockSpec((db, D_SG), lambda i: (0, Z_GATE_B)),
            const2((1, D_SG)), const2((1, D_SG)), const2((1, D_SG)), const2((1, D_SG)),
        ],
        out_specs=[pl.BlockSpec((db, D_SG), lambda i: (0, 0)),
                   pl.BlockSpec((db, D_SG), lambda i: (0, 0))],
        out_shape=[jax.ShapeDtypeStruct((db, D_SG), BF16),
                   jax.ShapeDtypeStruct((db, D_SG), F32)],
        name="sgate_sample",
    )(zs, zs, zs, ln_g, ln_b, w0, b0)


def _merge_kernel(x_ref, a_ref, b_ref, ga0_ref, ga1_ref, gb0_ref, gb1_ref,
                  wpa_ref, wpb_ref, wo_ref, np_ref, o_ref):
    ya = jnp.dot(a_ref[...], wpa_ref[...], preferred_element_type=F32)
    yb = jnp.dot(b_ref[...], wpb_ref[...], preferred_element_type=F32)
    half = D_MODEL // 2
    sig = lambda r: jax.nn.sigmoid(r[...].astype(F32))
    m_lo = sig(ga0_ref) * ya[:, :half] + sig(gb0_ref) * yb[:, :half]
    m_hi = sig(ga1_ref) * ya[:, half:] + sig(gb1_ref) * yb[:, half:]
    m = jnp.concatenate([m_lo, m_hi], axis=1).astype(BF16)
    mo = jnp.dot(m, wo_ref[...], preferred_element_type=F32)
    ms = jnp.mean(mo * mo, axis=-1, keepdims=True)
    o_ref[...] = x_ref[...] + mo * lax.rsqrt(ms + EPS) * np_ref[...]


def _merge(x, a_in, b_in, z, wpa_bf, wpb_bf, wo_bf, norm_post, *, tm):
    m = x.shape[0]
    resident = lambda shp: pl.BlockSpec(shp, lambda i: (0, 0), pipeline_mode=pl.Buffered(1))
    zcol = lambda jcol: pl.BlockSpec((tm, COL_BLOCK), lambda i: (i, jcol))
    return pl.pallas_call(
        _merge_kernel,
        grid=(m // tm,),
        in_specs=[
            pl.BlockSpec((tm, D_MODEL), lambda i: (i, 0)),
            pl.BlockSpec((tm, D_ATTN), lambda i: (i, 0)),
            pl.BlockSpec((tm, D_SG), lambda i: (i, 0)),
            zcol(Z_GA), zcol(Z_GA + 1), zcol(Z_GB), zcol(Z_GB + 1),
            resident((D_ATTN, D_MODEL)), resident((D_SG, D_MODEL)), resident((D_MODEL, D_MODEL)),
            resident((1, D_MODEL)),
        ],
        out_specs=pl.BlockSpec((tm, D_MODEL), lambda i: (i, 0)),
        out_shape=jax.ShapeDtypeStruct((m, D_MODEL), F32),
        compiler_params=pltpu.CompilerParams(dimension_semantics=("parallel",),
                                             vmem_limit_bytes=56 * MIB),
        name="merge",
    )(x, a_in, b_in, z, z, z, z, wpa_bf, wpb_bf, wo_bf, norm_post)


def _sgate_merge_kernel(x_ref, a_ref, u_ref, vsg_ref, gbt_ref, lng_ref, lnb_ref, ws_ref, bias_ref,
                        ga0_ref, ga1_ref, gb0_ref, gb1_ref, wpa_ref, wpb_ref, wo_ref, np_ref,
                        o_ref, b_ref):
    _sgate_prompt_kernel(u_ref, vsg_ref, gbt_ref, lng_ref, lnb_ref, ws_ref, bias_ref, b_ref)
    _merge_kernel(x_ref, a_ref, b_ref, ga0_ref, ga1_ref, gb0_ref, gb1_ref,
                  wpa_ref, wpb_ref, wo_ref, np_ref, o_ref)


def _sgate_merge(x, a_in, z, ln_g, ln_b, w_s, bias, wpa_bf, wpb_bf, wo_bf, norm_post, *, tm):
    m = x.shape[0]
    resident = lambda shp: pl.BlockSpec(shp, lambda i: (0,) * len(shp),
                                        pipeline_mode=pl.Buffered(1))
    zcol = lambda jcol: pl.BlockSpec((tm, COL_BLOCK), lambda i: (i, jcol))
    return pl.pallas_call(
        _sgate_merge_kernel,
        grid=(m // tm,),
        in_specs=[
            pl.BlockSpec((tm, D_MODEL), lambda i: (i, 0)),
            pl.BlockSpec((tm, D_ATTN), lambda i: (i, 0)),
            zcol(Z_U), zcol(Z_VSG), zcol(Z_GATE_B),
            resident((1, D_SG)), resident((1, D_SG)),
            resident((N_GROUPS, CHUNK, CHUNK)), resident((CHUNK, D_SG)),
            zcol(Z_GA), zcol(Z_GA + 1), zcol(Z_GB), zcol(Z_GB + 1),
            resident((D_ATTN, D_MODEL)), resident((D_SG, D_MODEL)), resident((D_MODEL, D_MODEL)),
            resident((1, D_MODEL)),
        ],
        out_specs=pl.BlockSpec((tm, D_MODEL), lambda i: (i, 0)),
        out_shape=jax.ShapeDtypeStruct((m, D_MODEL), F32),
        scratch_shapes=[pltpu.VMEM((tm, D_SG), BF16)],
        compiler_params=pltpu.CompilerParams(dimension_semantics=("parallel",),
                                             vmem_limit_bytes=56 * MIB),
        name="sgate_merge",
    )(x, a_in, z, z, z, ln_g, ln_b, w_s, bias, z, z, z, z, wpa_bf, wpb_bf, wo_bf, norm_post)


def _rope_tables(pos):
    half = DH // 2
    inv = ROPE_THETA ** (-(jnp.arange(half, dtype=F32) * 2.0 / DH))
    ang = pos.astype(F32)[:, None] * inv[None, :]
    cos = jnp.cos(ang)
    sin = jnp.sin(ang)
    reps = LANES // DH
    cos_t = jnp.tile(jnp.concatenate([cos, cos], axis=1), (1, reps))
    sin_t = jnp.tile(jnp.concatenate([-sin, sin], axis=1), (1, reps))
    return cos_t, sin_t


def kernel(x_prompt, x_sample, cache_k, cache_v, page_table, norm_pre, w_in, lam_q1, lam_k1,
           lam_q2, lam_k2, sub_g, w_pa, ln_g, ln_b, w_s, b_s, w_pb, w_o, norm_post):
    b, s, _ = x_prompt.shape
    db, t, _ = x_sample.shape
    assert b == 1 and t == 1 and norm_pre.shape[0] == 1
    n_past = page_table.shape[1] * cache_k.shape[2]

    wpa_bf = w_pa[0].astype(BF16)
    wpb_bf = w_pb[0].astype(BF16)
    wo_bf = w_o[0].astype(BF16)
    lams = (lam_q1, lam_k1, lam_q2, lam_k2)

    xs = x_sample.reshape(db, D_MODEL)
    cos_s, sin_s = _rope_tables(jnp.full((db,), n_past, dtype=jnp.int32))
    z_s, k_s, v_s, _, _, w_in_bf = _inproj(xs, norm_pre, w_in[0], cos_s, sin_s, tm=db,
                                           z_dtype=F32)
    sample_ops = _sample_operands(page_table, z_s[:, :D_ATTN], k_s, v_s,
                                  z_s[:, Z_GATE_A * COL_BLOCK:(Z_GATE_A + 1) * COL_BLOCK], cache_k[0], cache_v[0])
    n_pages = page_table.shape[1]

    xp = x_prompt.reshape(s, D_MODEL)
    cos_p, sin_p = _rope_tables(jnp.arange(s, dtype=jnp.int32))
    z_p, k_p, v_p, kh_p, vh_p = _inproj(xp, norm_pre, w_in_bf, cos_p, sin_p, tm=1024,
                                        z_dtype=BF16)
    a_p, a_s, n_seq = _fused_attn(z_p, kh_p, vh_p, sample_ops, *lams, sub_g, tq=512,
                                  n_pages=n_pages, pages=SAMPLE_PAGES)
    bias_p = jnp.repeat(jnp.transpose(b_s[0]), SG_GROUP, axis=1)
    y_p = _sgate_merge(xp, a_p, z_p, ln_g, ln_b, w_s[0], bias_p, wpa_bf, wpb_bf, wo_bf,
                       norm_post, tm=256)

    if n_seq < db:
        a_rest = _sample_attn(sample_ops, *lams, sub_g, first=n_seq, count=db - n_seq,
                              n_pages=n_pages, pages=SAMPLE_PAGES // 2)
        a_s = jnp.concatenate([a_s, a_rest], axis=0)
    a_s = a_s.reshape(db, D_ATTN)
    w0 = jnp.repeat(w_s[0, :, 0, 0], SG_GROUP)[None, :]
    b0 = jnp.repeat(b_s[0, :, 0], SG_GROUP)[None, :]
    b_sm, vn_s = _sgate_sample(z_s, ln_g, ln_b, w0, b0)
    y_s = _merge(xs, a_s, b_sm, z_s, wpa_bf, wpb_bf, wo_bf, norm_post, tm=db)

    return (y_p.reshape(b, s, D_MODEL),
            y_s.reshape(db, t, D_MODEL),
            k_p.reshape(1, b, s, N_HEADS, 2, DH),
            v_p.reshape(1, b, s, N_HEADS, DV),
            k_s.reshape(1, db, t, N_HEADS, 2, DH),
            v_s.reshape(1, db, t, N_HEADS, DV),
            vn_s.reshape(1, db, t, D_SG))
```
